```python
import math
import jax, jax.numpy as jnp
from jax import lax
import numpy as np

D_MODEL = 2048
BATCH = 2
SEQ = 8192
DEPTH = 1
DEC_BATCH = 32
DEC_SEQ = 1
PAST_LEN = 16384
PAGE_SIZE = 128

HEAD_DIM = 128
D_MIX = D_MODEL
D_DIFF = D_MIX // 2
D_MOBA = D_MIX - D_DIFF
H_DIFF = D_DIFF // HEAD_DIM
H_MOBA = D_MOBA // HEAD_DIM
DQK_DIFF = HEAD_DIM // 2
MOBA_BLOCK = 256
MOBA_TOPK = 3
D_FF = 4 * D_MODEL
Q_BLOCK = 128
MOBA_Q_BLOCK = 32
EPS = 1e-6
NEG_INF = float('-inf')
F32 = jnp.float32

kernel_name = 'hymba_diffattn_moba_decode_step'


def _rms(x, g):
    xf = x.astype(F32)
    y = xf * lax.rsqrt(jnp.mean(xf * xf, axis=-1, keepdims=True) + EPS)
    return (y * g.astype(F32)).astype(x.dtype)


def _alibi_slopes(n):
    return jnp.exp2(-8.0 * jnp.arange(1, n + 1, dtype=F32) / n)


def _project(x, norm_g, w_in, qn_diff_g, kn_diff_g, qn_moba_g, kn_moba_g):
    b, t, _ = x.shape
    z = jnp.einsum('btd,de->bte', _rms(x, norm_g), w_in)
    qd, kd, vd, qm, km, vm = jnp.split(
        z, [D_DIFF, 2 * D_DIFF, 3 * D_DIFF, 3 * D_DIFF + D_MOBA, 3 * D_DIFF + 2 * D_MOBA], axis=-1)
    qd = _rms(qd.reshape(b, t, H_DIFF, 2, DQK_DIFF), qn_diff_g)
    kd = _rms(kd.reshape(b, t, H_DIFF, 2, DQK_DIFF), kn_diff_g)
    vd = vd.reshape(b, t, H_DIFF, HEAD_DIM)
    qm = _rms(qm.reshape(b, t, H_MOBA, HEAD_DIM), qn_moba_g)
    km = _rms(km.reshape(b, t, H_MOBA, HEAD_DIM), kn_moba_g)
    vm = vm.reshape(b, t, H_MOBA, HEAD_DIM)
    return qd, kd, vd, qm, km, vm


def _diff_attn_prompt(qd, kd, vd, lam, slopes):
    b, s, h, _, _ = qd.shape
    n_qb = s // Q_BLOCK
    scale = DQK_DIFF ** -0.5
    pos_k = jnp.arange(s)
    kf = kd.astype(F32)
    vf = vd.astype(F32)
    q_blocks = jnp.moveaxis(qd.astype(F32).reshape(b, n_qb, Q_BLOCK, h, 2, DQK_DIFF), 1, 0)

    def one_block(args):
        qi, i = args
        pos_q = i * Q_BLOCK + jnp.arange(Q_BLOCK)
        dist = (pos_q[:, None] - pos_k[None, :]).astype(F32)
        sc = jnp.einsum('bqhcd,bkhcd->bhcqk', qi, kf) * scale - slopes[:, None, None, None] * dist
        sc = jnp.where(dist >= 0, sc, NEG_INF)
        p = jax.nn.softmax(sc, axis=-1)
        o = jnp.einsum('bhcqk,bkhd->bqhcd', p, vf)
        return o[..., 0, :] - lam * o[..., 1, :]

    out = lax.map(one_block, (q_blocks, jnp.arange(n_qb)))
    return jnp.moveaxis(out, 0, 1).reshape(b, s, h, HEAD_DIM)


def _moba_prompt(qm, km, vm, slopes):
    b, s, h, d = qm.shape
    n_kb = -(-s // MOBA_BLOCK)
    topk = min(MOBA_TOPK, n_kb)
    pad = n_kb * MOBA_BLOCK - s
    scale = HEAD_DIM ** -0.5
    padw = ((0, 0), (0, 0), (0, pad), (0, 0))
    kb = jnp.pad(jnp.transpose(km, (0, 2, 1, 3)).astype(F32), padw).reshape(b, h, n_kb, MOBA_BLOCK, d)
    vb = jnp.pad(jnp.transpose(vm, (0, 2, 1, 3)).astype(F32), padw).reshape(b, h, n_kb, MOBA_BLOCK, d)
    k_mean = jnp.mean(kb, axis=3)
    n_qb = s // MOBA_Q_BLOCK
    q_blocks = jnp.moveaxis(
        jnp.transpose(qm, (0, 2, 1, 3)).astype(F32).reshape(b, h, n_qb, MOBA_Q_BLOCK, d), 2, 0)
    bi = jnp.arange(b)[:, None, None, None]
    hi = jnp.arange(h)[None, :, None, None]
    rows = jnp.arange(MOBA_BLOCK)
    n_sel = topk * MOBA_BLOCK

    def one_block(args):
        qi, i = args
        pos_q = i * MOBA_Q_BLOCK + jnp.arange(MOBA_Q_BLOCK)
        c = (i * MOBA_Q_BLOCK) // MOBA_BLOCK
        gate = jnp.einsum('bhqd,bhnd->bhqn', qi, k_mean)
        gate = jnp.where(jnp.arange(n_kb) < c, gate, NEG_INF)
        _, sel = lax.top_k(gate, topk)
        sel_ok = jnp.arange(topk) < c
        k_sel = kb[bi, hi, sel]
        v_sel = vb[bi, hi, sel]
        pos_s = sel[..., None] * MOBA_BLOCK + rows
        s_sel = (jnp.einsum('bhqd,bhqjrd->bhqjr', qi, k_sel) * scale
                 - slopes[:, None, None, None] * (pos_q[:, None, None] - pos_s))
        s_sel = jnp.where(sel_ok[:, None], s_sel, NEG_INF).reshape(b, h, MOBA_Q_BLOCK, n_sel)
        k_own = lax.dynamic_index_in_dim(kb, c, axis=2, keepdims=False)
        v_own = lax.dynamic_index_in_dim(vb, c, axis=2, keepdims=False)
        dist_o = pos_q[:, None] - (c * MOBA_BLOCK + rows)[None, :]
        s_own = jnp.einsum('bhqd,bhrd->bhqr', qi, k_own) * scale - slopes[:, None, None] * dist_o
        s_own = jnp.where(dist_o >= 0, s_own, NEG_INF)
        p = jax.nn.softmax(jnp.concatenate([s_sel, s_own], axis=-1), axis=-1)
        o = (jnp.einsum('bhqj,bhqjd->bhqd', p[..., :n_sel], v_sel.reshape(b, h, MOBA_Q_BLOCK, n_sel, d))
             + jnp.einsum('bhqr,bhrd->bhqd', p[..., n_sel:], v_own))
        return o

    out = lax.map(one_block, (q_blocks, jnp.arange(n_qb)))
    return jnp.transpose(out, (1, 0, 3, 2, 4)).reshape(b, s, h, d)


def _sample_paged_sweep(qd, kd, vd, lam, slopes, page_table, layer, cache_k_diff, cache_v_diff, cache_k_moba):
    b, t, h, _, _ = qd.shape
    n_pages = PAST_LEN // PAGE_SIZE
    scale = DQK_DIFF ** -0.5
    pos_q = PAST_LEN + jnp.arange(t)
    qf = qd.astype(F32)

    def one_page(p):
        phys = page_table[:, p]
        kp = cache_k_diff[layer, phys].astype(F32).reshape(b, PAGE_SIZE, h, 2, DQK_DIFF)
        vp = cache_v_diff[layer, phys].astype(F32)
        dist = (pos_q[:, None] - (p * PAGE_SIZE + jnp.arange(PAGE_SIZE))[None, :]).astype(F32)
        sc = jnp.einsum('bthcd,bkhcd->bhctk', qf, kp) * scale - slopes[:, None, None, None] * dist
        m = jnp.max(sc, axis=-1)
        e = jnp.exp(sc - m[..., None])
        acc = jnp.einsum('bhctk,bkhd->bhctd', e, vp)
        ksum = jnp.sum(cache_k_moba[layer, phys].astype(F32), axis=1)
        return m, jnp.sum(e, axis=-1), acc, ksum

    m_p, l_p, acc_p, ksum_pages = lax.map(one_page, jnp.arange(n_pages))
    dist_n = (pos_q[:, None] - pos_q[None, :]).astype(F32)
    sc = jnp.einsum('bthcd,bshcd->bhcts', qf, kd.astype(F32)) * scale - slopes[:, None, None, None] * dist_n
    sc = jnp.where(dist_n >= 0, sc, NEG_INF)
    m_n = jnp.max(sc, axis=-1)
    e_n = jnp.exp(sc - m_n[..., None])
    acc_n = jnp.einsum('bhcts,bshd->bhctd', e_n, vd.astype(F32))
    m_all = jnp.maximum(jnp.max(m_p, axis=0), m_n)
    w_p = jnp.exp(m_p - m_all)
    w_n = jnp.exp(m_n - m_all)
    denom = jnp.sum(w_p * l_p, axis=0) + w_n * jnp.sum(e_n, axis=-1)
    num = jnp.einsum('pbhct,pbhctd->bhctd', w_p, acc_p) + w_n[..., None] * acc_n
    o = num / denom[..., None]
    o = o[:, :, 0] - lam * o[:, :, 1]
    return jnp.transpose(o, (0, 2, 1, 3)), ksum_pages


def _moba_sample(qm, km, vm, ksum_pages, page_table, layer, cache_k_moba, cache_v_moba, slopes):
    b, t, h, d = qm.shape
    n_pages = PAST_LEN // PAGE_SIZE
    ppb = MOBA_BLOCK // PAGE_SIZE
    n_kb = -(-(PAST_LEN + t) // MOBA_BLOCK)
    topk = min(MOBA_TOPK, n_kb)
    scale = HEAD_DIM ** -0.5
    pos_q = PAST_LEN + jnp.arange(t)
    c_q = pos_q // MOBA_BLOCK
    page_blk = (jnp.arange(n_pages) * PAGE_SIZE) // MOBA_BLOCK
    blk_sum = jax.ops.segment_sum(ksum_pages, page_blk, num_segments=n_kb)
    blk_sum = blk_sum + jax.ops.segment_sum(jnp.moveaxis(km.astype(F32), 1, 0), c_q, num_segments=n_kb)
    k_mean = jnp.transpose(blk_sum, (1, 2, 0, 3)) / MOBA_BLOCK
    qt = jnp.transpose(qm, (0, 2, 1, 3)).astype(F32)
    gate = jnp.einsum('bhtd,bhnd->bhtn', qt, k_mean)
    gate = jnp.where(jnp.arange(n_kb)[None, :] < c_q[:, None], gate, NEG_INF)
    _, sel = lax.top_k(gate, topk)
    sel_ok = jnp.arange(topk)[None, :] < c_q[:, None]
    rows = jnp.arange(PAGE_SIZE)
    bi = jnp.arange(b)[:, None, None, None, None]
    hi = jnp.arange(h)[None, :, None, None, None, None]
    lp = sel[..., None] * ppb + jnp.arange(ppb)
    phys = page_table[bi, jnp.clip(lp, 0, n_pages - 1)]
    k_sel = cache_k_moba[layer, phys[..., None], rows, hi].astype(F32)
    v_sel = cache_v_moba[layer, phys[..., None], rows, hi].astype(F32)
    pos_s = lp[..., None] * PAGE_SIZE + rows
    ok_s = (pos_s < PAST_LEN) & sel_ok[None, None, :, :, None, None]
    s_sel = (jnp.einsum('bhtd,bhtjprd->bhtjpr', qt, k_sel) * scale
             - slopes[:, None, None, None, None] * (pos_q[:, None, None, None] - pos_s))
    s_sel = jnp.where(ok_s, s_sel, NEG_INF)
    lpo = c_q[:, None] * ppb + jnp.arange(ppb)
    phys_o = page_table[:, jnp.clip(lpo, 0, n_pages - 1)]
    hi_o = jnp.arange(h)[None, :, None, None, None]
    k_own = cache_k_moba[layer, phys_o[:, None, :, :, None], rows, hi_o].astype(F32)
    v_own = cache_v_moba[layer, phys_o[:, None, :, :, None], rows, hi_o].astype(F32)
    pos_o = lpo[..., None] * PAGE_SIZE + rows
    s_own = (jnp.einsum('bhtd,bhtprd->bhtpr', qt, k_own) * scale
             - slopes[:, None, None, None] * (pos_q[:, None, None] - pos_o))
    s_own = jnp.where(pos_o < PAST_LEN, s_own, NEG_INF)
    kn = jnp.transpose(km, (0, 2, 1, 3)).astype(F32)
    vn = jnp.transpose(vm, (0, 2, 1, 3)).astype(F32)
    dist_n = pos_q[:, None] - pos_q[None, :]
    in_sel = jnp.any((sel[..., None] == c_q) & sel_ok[None, None, :, :, None], axis=3)
    ok_n = (dist_n >= 0) & ((c_q[None, :] == c_q[:, None]) | in_sel)
    s_new = jnp.einsum('bhtd,bhsd->bhts', qt, kn) * scale - slopes[:, None, None] * dist_n
    s_new = jnp.where(ok_n, s_new, NEG_INF)
    n_sel = topk * ppb * PAGE_SIZE
    n_own = ppb * PAGE_SIZE
    p = jax.nn.softmax(jnp.concatenate(
        [s_sel.reshape(b, h, t, n_sel), s_own.reshape(b, h, t, n_own), s_new], axis=-1), axis=-1)
    o = (jnp.einsum('bhtj,bhtjd->bhtd', p[..., :n_sel], v_sel.reshape(b, h, t, n_sel, d))
         + jnp.einsum('bhtj,bhtjd->bhtd', p[..., n_sel:n_sel + n_own], v_own.reshape(b, h, t, n_own, d))
         + jnp.einsum('bhts,bhsd->bhtd', p[..., n_sel + n_own:], vn))
    return jnp.transpose(o, (0, 2, 1, 3))


def _merge_and_mlp(x, od, om, lam_init, subln_g, out_norm_g, w_out, mlp_norm_g, w_up, w_down):
    b, t, _ = x.shape
    od = _rms(od.astype(x.dtype), subln_g) * (1.0 - lam_init)
    om = _rms(om.astype(x.dtype), out_norm_g)
    mix = jnp.concatenate([od.reshape(b, t, D_DIFF), om.reshape(b, t, D_MOBA)], axis=-1)
    h = x + jnp.einsum('bte,ed->btd', mix, w_out)
    u = jnp.square(jax.nn.relu(jnp.einsum('btd,df->btf', _rms(h, mlp_norm_g), w_up)))
    return h + jnp.einsum('btf,fd->btd', u, w_down)


def setup_inputs(seed: int = 0) -> dict:
    key = jax.random.key(seed)
    ks = jax.random.split(key, 24)
    n_pages = PAST_LEN // PAGE_SIZE
    n_used = DEC_BATCH * n_pages
    n_pool = n_used + max(1, n_used // 4)

    def nrm(k, shape, scale=1.0):
        return jax.random.normal(k, shape, F32) * scale

    def gain(k, shape):
        return 1.0 + 0.02 * jax.random.normal(k, shape, F32)

    page_table = jax.random.permutation(ks[6], n_pool)[:n_used].reshape(DEC_BATCH, n_pages).astype(jnp.int32)
    return {
        'x_prompt': nrm(ks[0], (BATCH, SEQ, D_MODEL)),
        'x_sample': nrm(ks[1], (DEC_BATCH, DEC_SEQ, D_MODEL)),
        'cache_k_diff': nrm(ks[2], (DEPTH, n_pool, PAGE_SIZE, H_DIFF, HEAD_DIM)),
        'cache_v_diff': nrm(ks[3], (DEPTH, n_pool, PAGE_SIZE, H_DIFF, HEAD_DIM)),
        'cache_k_moba': nrm(ks[4], (DEPTH, n_pool, PAGE_SIZE, H_MOBA, HEAD_DIM)),
        'cache_v_moba': nrm(ks[5], (DEPTH, n_pool, PAGE_SIZE, H_MOBA, HEAD_DIM)),
        'page_table': page_table,
        'attn_norm_g': gain(ks[7], (DEPTH, D_MODEL)),
        'w_in': nrm(ks[8], (DEPTH, D_MODEL, 3 * D_DIFF + 3 * D_MOBA), D_MODEL ** -0.5),
        'qn_diff_g': gain(ks[9], (DEPTH, DQK_DIFF)),
        'kn_diff_g': gain(ks[10], (DEPTH, DQK_DIFF)),
        'qn_moba_g': gain(ks[11], (DEPTH, HEAD_DIM)),
        'kn_moba_g': gain(ks[12], (DEPTH, HEAD_DIM)),
        'lambda_q1': nrm(ks[13], (DEPTH, DQK_DIFF), 0.1),
        'lambda_k1': nrm(ks[14], (DEPTH, DQK_DIFF), 0.1),
        'lambda_q2': nrm(ks[15], (DEPTH, DQK_DIFF), 0.1),
        'lambda_k2': nrm(ks[16], (DEPTH, DQK_DIFF), 0.1),
        'subln_diff_g': gain(ks[17], (DEPTH, HEAD_DIM)),
        'out_norm_moba_g': gain(ks[18], (DEPTH, HEAD_DIM)),
        'w_out': nrm(ks[19], (DEPTH, D_MIX, D_MODEL), D_MIX ** -0.5),
        'mlp_norm_g': gain(ks[20], (DEPTH, D_MODEL)),
        'w_up': nrm(ks[21], (DEPTH, D_MODEL, D_FF), D_MODEL ** -0.5),
        'w_down': nrm(ks[22], (DEPTH, D_FF, D_MODEL), D_FF ** -0.5),
    }


def reference(x_prompt, x_sample, cache_k_diff, cache_v_diff, cache_k_moba, cache_v_moba, page_table,
              attn_norm_g, w_in, qn_diff_g, kn_diff_g, qn_moba_g, kn_moba_g,
              lambda_q1, lambda_k1, lambda_q2, lambda_k2, subln_diff_g, out_norm_moba_g, w_out,
              mlp_norm_g, w_up, w_down):
    slopes_diff = _alibi_slopes(H_DIFF)
    slopes_moba = _alibi_slopes(H_MOBA)
    xp, xs = x_prompt, x_sample
    bp, sp = xp.shape[0], xp.shape[1]
    bs, ss = xs.shape[0], xs.shape[1]
    kdp, vdp, kmp, vmp, kds, vds, kms, vms = [], [], [], [], [], [], [], []
    for l in range(DEPTH):
        lam_init = 0.8 - 0.6 * math.exp(-0.3 * l)
        lam = (jnp.exp(jnp.sum(lambda_q1[l].astype(F32) * lambda_k1[l].astype(F32)))
               - jnp.exp(jnp.sum(lambda_q2[l].astype(F32) * lambda_k2[l].astype(F32))) + lam_init)
        proj = (attn_norm_g[l], w_in[l], qn_diff_g[l], kn_diff_g[l], qn_moba_g[l], kn_moba_g[l])
        post = (lam_init, subln_diff_g[l], out_norm_moba_g[l], w_out[l], mlp_norm_g[l], w_up[l], w_down[l])
        qd, kd, vd, qm, km, vm = _project(xp, *proj)
        od = _diff_attn_prompt(qd, kd, vd, lam, slopes_diff)
        om = _moba_prompt(qm, km, vm, slopes_moba)
        kdp.append(kd.reshape(bp, sp, H_DIFF, HEAD_DIM))
        vdp.append(vd)
        kmp.append(km)
        vmp.append(vm)
        xp = _merge_and_mlp(xp, od, om, *post)
        qd, kd, vd, qm, km, vm = _project(xs, *proj)
        od, ksum_pages = _sample_paged_sweep(qd, kd, vd, lam, slopes_diff, page_table, l,
                                             cache_k_diff, cache_v_diff, cache_k_moba)
        om = _moba_sample(qm, km, vm, ksum_pages, page_table, l, cache_k_moba, cache_v_moba, slopes_moba)
        kds.append(kd.reshape(bs, ss, H_DIFF, HEAD_DIM))
        vds.append(vd)
        kms.append(km)
        vms.append(vm)
        xs = _merge_and_mlp(xs, od, om, *post)
    y_prompt = xp
    y_sample = xs
    new_k_diff_prompt = jnp.stack(kdp)
    new_v_diff_prompt = jnp.stack(vdp)
    new_k_moba_prompt = jnp.stack(kmp)
    new_v_moba_prompt = jnp.stack(vmp)
    new_k_diff_sample = jnp.stack(kds)
    new_v_diff_sample = jnp.stack(vds)
    new_k_moba_sample = jnp.stack(kms)
    new_v_moba_sample = jnp.stack(vms)
    return (y_prompt, y_sample, new_k_diff_prompt, new_v_diff_prompt, new_k_moba_prompt, new_v_moba_prompt,
            new_k_diff_sample, new_v_diff_sample, new_k_moba_sample, new_v_moba_sample)
```

```python
import functools
import math

import jax
import jax.numpy as jnp
from jax import lax
from jax.experimental import pallas as pl
from jax.experimental.pallas import tpu as pltpu

F32 = jnp.float32
BF16 = jnp.bfloat16

D_MODEL = 2048
HEAD_DIM = 128
N_HEADS = 8
D_GROUP = N_HEADS * HEAD_DIM
DQK_DIFF = HEAD_DIM // 2
MOBA_BLOCK = 256
MOBA_TOPK = 3
PAGE_SIZE = 128
D_FF = 4 * D_MODEL
EPS = 1e-6
NEG_INF = float("-inf")
LAM_INIT = 0.8 - 0.6 * math.exp(-0.3 * 0)

VMEM_LIMIT = 56 * 1024 * 1024


def _dot(a, b):
    return jnp.dot(a, b, preferred_element_type=F32)


def _dot_nt(a, b):
    return lax.dot_general(a, b, (((1,), (1,)), ((), ())), preferred_element_type=F32)


def _params(*sem):
    return pltpu.CompilerParams(dimension_semantics=sem, vmem_limit_bytes=VMEM_LIMIT)


def _proj_kernel(x_ref, g_ref, w_ref, gain_ref, gainb_ref, bmat_ref,
                 kd_ref, vd_ref, qm_ref, km_ref, vm_ref, bf_ref, xn_ref):
    j = pl.program_id(1)

    @pl.when(j == 0)
    def _():
        x = x_ref[...]
        ms = jnp.mean(x * x, axis=-1, keepdims=True)
        xn_ref[...] = (x * lax.rsqrt(ms + EPS) * g_ref[...]).astype(BF16)

    z = _dot(xn_ref[...], w_ref[...])
    is_v = (j % 3) == 2
    bmat = bmat_ref[0]
    ys, ybs = [], []
    for hh in range(N_HEADS):
        zc = z[:, hh * HEAD_DIM:(hh + 1) * HEAD_DIM]
        ms = _dot((zc * zc).astype(BF16), bmat)
        rs = jnp.where(is_v, 1.0, lax.rsqrt(ms + EPS))
        zr = zc * rs
        ys.append(zr * gain_ref[0])
        ybs.append((zr * gainb_ref[0]).astype(BF16))
    y = jnp.concatenate(ys, axis=1)
    bf_ref[...] = jnp.concatenate(ybs, axis=1)

    for jj, ref in ((1, kd_ref), (2, vd_ref), (3, qm_ref), (4, km_ref), (5, vm_ref)):
        @pl.when(j == jj)
        def _(ref=ref):
            ref[...] = y


def _proj(x, g, w_bf, gains, gains_b, bmats, tm):
    rows = x.shape[0]
    n_i = rows // tm
    f32_out = jax.ShapeDtypeStruct((rows, D_GROUP), F32)
    f32_spec = pl.BlockSpec((tm, D_GROUP), lambda i, j: (i, 0))
    return pl.pallas_call(
        _proj_kernel,
        grid=(n_i, 6),
        in_specs=[
            pl.BlockSpec((tm, D_MODEL), lambda i, j: (i, 0)),
            pl.BlockSpec((1, D_MODEL), lambda i, j: (0, 0)),
            pl.BlockSpec((D_MODEL, D_GROUP), lambda i, j: (0, j)),
            pl.BlockSpec((1, 1, HEAD_DIM), lambda i, j: (j, 0, 0)),
            pl.BlockSpec((1, 1, HEAD_DIM), lambda i, j: (j, 0, 0)),
            pl.BlockSpec((1, HEAD_DIM, HEAD_DIM), lambda i, j: (j // 3, 0, 0)),
        ],
        out_specs=[f32_spec] * 5 + [pl.BlockSpec((tm, D_GROUP), lambda i, j: (i, j))],
        out_shape=[f32_out] * 5 + [jax.ShapeDtypeStruct((rows, 6 * D_GROUP), BF16)],
        scratch_shapes=[pltpu.VMEM((tm, D_MODEL), BF16)],
        compiler_params=_params("arbitrary", "arbitrary"),
        name="proj",
    )(x, g, w_bf, gains, gains_b, bmats)


def _kmean_kernel(k_ref, o_ref, *, nb):
    k = k_ref[...].reshape(nb, MOBA_BLOCK, D_GROUP)
    o_ref[...] = jnp.sum(k, axis=1) * (1.0 / MOBA_BLOCK)


def _kmean(km, nb=8):
    rows = km.shape[0]
    return pl.pallas_call(
        functools.partial(_kmean_kernel, nb=nb),
        grid=(rows // (nb * MOBA_BLOCK),),
        in_specs=[pl.BlockSpec((nb * MOBA_BLOCK, D_GROUP), lambda i: (i, 0))],
        out_specs=pl.BlockSpec((nb, D_GROUP), lambda i: (i, 0)),
        out_shape=jax.ShapeDtypeStruct((rows // MOBA_BLOCK, D_GROUP), F32),
        compiler_params=_params("arbitrary"),
        name="kmean",
    )(km)


def _lam(lam_ref):
    l = lam_ref[...]
    a = jnp.sum(l[0:1] * l[1:2], axis=-1, keepdims=True)
    b = jnp.sum(l[2:3] * l[3:4], axis=-1, keepdims=True)
    return jnp.exp(a) - jnp.exp(b) + LAM_INIT


def _head_rms(o, gain):
    return o * lax.rsqrt(jnp.mean(o * o, axis=-1, keepdims=True) + EPS) * gain


def _diff_kernel(slopes_ref, q_ref, k_ref, v_ref, lam_ref, g_ref, o_ref, *, tq, tk):
    h = pl.program_id(1)
    qi = pl.program_id(2)
    slope = slopes_ref[h]
    q = q_ref[...]
    lane = lax.broadcasted_iota(jnp.int32, (1, HEAD_DIM), 1)
    zero = jnp.zeros_like(q)
    q1 = jnp.where(lane < DQK_DIFF, q, zero)
    q2 = jnp.where(lane >= DQK_DIFF, q, zero)
    q_start = qi * tq
    col = lax.broadcasted_iota(jnp.int32, (1, tk), 1)

    def tile(kt, carry, masked):
        m1, l1, a1, m2, l2, a2 = carry
        k_start = pl.multiple_of(kt * tk, tk)
        k = k_ref[pl.ds(k_start, tk), :]
        v = v_ref[pl.ds(k_start, tk), :]
        bias = slope * (col + (k_start - q_start)).astype(F32)
        if masked:
            row = lax.broadcasted_iota(jnp.int32, (tq, tk), 0)
            ok = (row + q_start) >= (col + k_start)

        def one(qc, m, l, a):
            s = _dot_nt(qc, k) + bias
            if masked:
                s = jnp.where(ok, s, NEG_INF)
            m_new = jnp.maximum(m, jnp.max(s, axis=-1, keepdims=True))
            alpha = jnp.exp(m - m_new)
            p = jnp.exp(s - m_new)
            l = alpha * l + jnp.sum(p, axis=-1, keepdims=True)
            a = alpha * a + _dot(p.astype(BF16), v)
            return m_new, l, a

        m1, l1, a1 = one(q1, m1, l1, a1)
        m2, l2, a2 = one(q2, m2, l2, a2)
        return m1, l1, a1, m2, l2, a2

    m0 = jnp.full((tq, 1), NEG_INF, F32)
    l0 = jnp.zeros((tq, 1), F32)
    a0 = jnp.zeros((tq, HEAD_DIM), F32)
    n_full = q_start // tk
    carry = lax.fori_loop(0, n_full, lambda kt, c: tile(kt, c, False), (m0, l0, a0, m0, l0, a0))
    m1, l1, a1, m2, l2, a2 = tile(n_full, carry, True)
    o = a1 / l1 - _lam(lam_ref) * (a2 / l2)
    o_ref[...] = (_head_rms(o, g_ref[...]) * (1.0 - LAM_INIT)).astype(BF16)


def _diff_attn(slopes, qkv_bf, lam4, subln_g, batch, seq, tq=512, tk=512):
    nq = seq // tq
    return pl.pallas_call(
        functools.partial(_diff_kernel, tq=tq, tk=tk),
        grid=(batch, N_HEADS, nq),
        in_specs=[
            pl.BlockSpec(memory_space=pltpu.SMEM),
            pl.BlockSpec((tq, HEAD_DIM), lambda b, h, i: (b * nq + i, h)),
            pl.BlockSpec((seq, HEAD_DIM), lambda b, h, i: (b, N_HEADS + h)),
            pl.BlockSpec((seq, HEAD_DIM), lambda b, h, i: (b, 2 * N_HEADS + h)),
            pl.BlockSpec((4, DQK_DIFF), lambda b, h, i: (0, 0)),
            pl.BlockSpec((1, HEAD_DIM), lambda b, h, i: (0, 0)),
        ],
        out_specs=pl.BlockSpec((tq, HEAD_DIM), lambda b, h, i: (b * nq + i, h)),
        out_shape=jax.ShapeDtypeStruct((batch * seq, D_GROUP), BF16),
        compiler_params=_params("arbitrary", "arbitrary", "arbitrary"),
        name="diff",
    )(slopes, qkv_bf, qkv_bf, qkv_bf, lam4, subln_g)


def _moba_kernel(slopes_ref, q_ref, qf_ref, k_ref, v_ref, kmean_ref, g_ref, o_ref, *, n_kb):
    h = pl.program_id(1)
    c = pl.program_id(2)
    blk = MOBA_BLOCK
    slope = slopes_ref[h]
    q = q_ref[...]

    gate = lax.dot_general(qf_ref[...], kmean_ref[...], (((1,), (1,)), ((), ())),
                           precision=lax.Precision.HIGHEST, preferred_element_type=F32)
    bidx_i = lax.broadcasted_iota(jnp.int32, (1, n_kb), 1)
    bidx = bidx_i.astype(F32)
    gate = jnp.where(bidx_i < c, gate, NEG_INF)
    sel = jnp.zeros((blk, n_kb), F32)
    for _ in range(MOBA_TOPK):
        mx = jnp.max(gate, axis=-1, keepdims=True)
        first = jnp.min(jnp.where(gate == mx, bidx, float(n_kb)), axis=-1, keepdims=True)
        hit = (bidx == first) & (mx > NEG_INF)
        sel = jnp.where(hit, 1.0, sel)
        gate = jnp.where(hit, NEG_INF, gate)
    sel_bf = sel.astype(BF16)

    col = lax.broadcasted_iota(jnp.int32, (1, blk), 1)
    brow = lax.broadcasted_iota(jnp.int32, (n_kb, blk), 0)

    def step(n, carry, own):
        m, l, a = carry
        k_start = pl.multiple_of(n * blk, blk)
        k = k_ref[pl.ds(k_start, blk), :]
        v = v_ref[pl.ds(k_start, blk), :]
        s = _dot_nt(q, k) + slope * (col + (n - c) * blk).astype(F32)
        if own:
            row = lax.broadcasted_iota(jnp.int32, (blk, blk), 0)
            ok = row >= col
        else:
            ok = _dot(sel_bf, jnp.where(brow == n, 1.0, 0.0).astype(BF16)) > 0.5
        s = jnp.where(ok, s, NEG_INF)
        m_new = jnp.maximum(m, jnp.max(s, axis=-1, keepdims=True))
        m_safe = jnp.where(m_new > NEG_INF, m_new, 0.0)
        alpha = jnp.exp(m - m_safe)
        p = jnp.exp(s - m_safe)
        l = alpha * l + jnp.sum(p, axis=-1, keepdims=True)
        a = alpha * a + _dot(p.astype(BF16), v)
        return m_new, l, a

    m0 = jnp.full((blk, 1), NEG_INF, F32)
    l0 = jnp.zeros((blk, 1), F32)
    a0 = jnp.zeros((blk, HEAD_DIM), F32)
    carry = lax.fori_loop(0, c, lambda n, cr: step(n, cr, False), (m0, l0, a0))
    m, l, a = step(c, carry, True)
    o_ref[...] = _head_rms(a / l, g_ref[...]).astype(BF16)


def _moba_attn(slopes, qkv_bf, qm_f32, kmean, out_g, batch, seq):
    n_kb = seq // MOBA_BLOCK
    return pl.pallas_call(
        functools.partial(_moba_kernel, n_kb=n_kb),
        grid=(batch, N_HEADS, n_kb),
        in_specs=[
            pl.BlockSpec(memory_space=pltpu.SMEM),
            pl.BlockSpec((MOBA_BLOCK, HEAD_DIM), lambda b, h, c: (b * n_kb + c, 3 * N_HEADS + h)),
            pl.BlockSpec((MOBA_BLOCK, HEAD_DIM), lambda b, h, c: (b * n_kb + c, h)),
            pl.BlockSpec((seq, HEAD_DIM), lambda b, h, c: (b, 4 * N_HEADS + h)),
            pl.BlockSpec((seq, HEAD_DIM), lambda b, h, c: (b, 5 * N_HEADS + h)),
            pl.BlockSpec((n_kb, HEAD_DIM), lambda b, h, c: (b, h)),
            pl.BlockSpec((1, HEAD_DIM), lambda b, h, c: (0, 0)),
        ],
        out_specs=pl.BlockSpec((MOBA_BLOCK, HEAD_DIM), lambda b, h, c: (b * n_kb + c, h)),
        out_shape=jax.ShapeDtypeStruct((batch * seq, D_GROUP), BF16),
        compiler_params=_params("arbitrary", "arbitrary", "arbitrary"),
        name="moba",
    )(slopes, qkv_bf, qm_f32, qkv_bf, qkv_bf, kmean, out_g)


def _sweep_kernel(pt_ref, *refs, pps, past_len):
    kd_refs = refs[0:pps]
    vd_refs = refs[pps:2 * pps]
    km_refs = refs[2 * pps:3 * pps]
    (q_ref, kn_ref, vn_ref, hmask_ref, tmpl_ref, slope_ref, lam_ref, g_ref,
     o_ref, bsum_ref, m_ref, l_ref, acc_ref) = refs[3 * pps:]
    pp = pl.program_id(1)
    nk = pps * PAGE_SIZE
    flat = PAGE_SIZE * N_HEADS

    qm = q_ref[...] * hmask_ref[...]

    @pl.when(pp == 0)
    def _():
        m_ref[...] = jnp.sum(qm * kn_ref[...], axis=-1, keepdims=True)
        l_ref[...] = jnp.ones_like(l_ref)
        acc_ref[...] = vn_ref[...]

    k = jnp.concatenate([r[...].reshape(flat, HEAD_DIM) for r in kd_refs], axis=0).astype(BF16)
    v = jnp.concatenate([r[...].reshape(flat, HEAD_DIM) for r in vd_refs], axis=0).astype(BF16)
    base = jnp.full((1, 1), pp * nk - past_len, jnp.int32).astype(F32)
    s = _dot_nt(qm.astype(BF16), k) + tmpl_ref[...] + slope_ref[...] * base
    m_old = m_ref[...]
    m_new = jnp.maximum(m_old, jnp.max(s, axis=-1, keepdims=True))
    alpha = jnp.exp(m_old - m_new)
    p = jnp.exp(s - m_new)
    l_ref[...] = alpha * l_ref[...] + jnp.sum(p, axis=-1, keepdims=True)
    acc_ref[...] = alpha * acc_ref[...] + _dot(p.astype(BF16), v)
    m_ref[...] = m_new

    ppb = MOBA_BLOCK // PAGE_SIZE
    sums = [jnp.sum(r[...], axis=0) for r in km_refs]
    for i in range(pps // ppb):
        bsum_ref[i] = sum(sums[i * ppb:(i + 1) * ppb])

    @pl.when(pp == pl.num_programs(1) - 1)
    def _():
        o16 = acc_ref[...] / l_ref[...]
        o = o16[0:N_HEADS] - _lam(lam_ref) * o16[N_HEADS:2 * N_HEADS]
        o_ref[...] = (_head_rms(o, g_ref[...]) * (1.0 - LAM_INIT)).astype(BF16)


def _sweep(page_table, slopes, ckd, cvd, ckm, q16, kn16, vn16, lam4, subln_g, past_len, pps=8):
    nb_s, n_pages = page_table.shape
    bps = pps * PAGE_SIZE // MOBA_BLOCK
    n_rows = 2 * N_HEADS
    n_cols = pps * PAGE_SIZE * N_HEADS

    row = jnp.arange(n_rows)[:, None]
    colj = jnp.arange(n_cols)[None, :]
    slope16 = slopes[row % N_HEADS]
    lane = jnp.arange(HEAD_DIM)[None, :]
    hmask = ((lane // DQK_DIFF) == (row // N_HEADS)).astype(F32)
    tmpl = jnp.where(colj % N_HEADS == row % N_HEADS,
                     slope16 * (colj // N_HEADS).astype(F32), NEG_INF)

    def page_spec(r):
        return pl.BlockSpec((None, None, PAGE_SIZE, N_HEADS, HEAD_DIM),
                            lambda b, pp, pt: (0, pt[b, pp * pps + r], 0, 0, 0))

    def const_spec(shape):
        return pl.BlockSpec(shape, lambda b, pp, pt: (0,) * len(shape))

    row_spec = pl.BlockSpec((None, n_rows, HEAD_DIM), lambda b, pp, pt: (b, 0, 0))
    return pl.pallas_call(
        functools.partial(_sweep_kernel, pps=pps, past_len=past_len),
        grid_spec=pltpu.PrefetchScalarGridSpec(
            num_scalar_prefetch=1,
            grid=(nb_s, n_pages // pps),
            in_specs=[page_spec(r) for r in range(pps)] * 3
            + [row_spec, row_spec, row_spec,
               const_spec((n_rows, HEAD_DIM)), const_spec((n_rows, n_cols)), const_spec((n_rows, 1)),
               const_spec((4, DQK_DIFF)), const_spec((1, HEAD_DIM))],
            out_specs=[pl.BlockSpec((None, N_HEADS, HEAD_DIM), lambda b, pp, pt: (b, 0, 0)),
                       pl.BlockSpec((None, bps, N_HEADS, HEAD_DIM), lambda b, pp, pt: (b, pp, 0, 0))],
            scratch_shapes=[pltpu.VMEM((n_rows, 1), F32), pltpu.VMEM((n_rows, 1), F32),
                            pltpu.VMEM((n_rows, HEAD_DIM), F32)],
        ),
        out_shape=[jax.ShapeDtypeStruct((nb_s, N_HEADS, HEAD_DIM), BF16),
                   jax.ShapeDtypeStruct((nb_s, n_pages * PAGE_SIZE // MOBA_BLOCK, N_HEADS, HEAD_DIM), F32)],
        compiler_params=_params("arbitrary", "arbitrary"),
        name="sweep",
    )(page_table, *([ckd] * pps), *([cvd] * pps), *([ckm] * pps), q16, kn16, vn16,
      hmask, tmpl, slope16, lam4, subln_g)


def _gate_kernel(q_ref, bsum_ref, sel_ref, *, n_blk):
    kmean = bsum_ref[...] * (1.0 / MOBA_BLOCK)
    gate = jnp.sum(kmean * q_ref[...], axis=-1, keepdims=True)
    bidx = lax.broadcasted_iota(jnp.int32, gate.shape, 0).astype(F32)
    for r in range(MOBA_TOPK):
        mx = jnp.max(gate, axis=0, keepdims=True)
        first = jnp.min(jnp.where(gate == mx, bidx, float(n_blk)), axis=0, keepdims=True)
        sel_ref[r] = first[0].astype(jnp.int32)
        gate = jnp.where(bidx == first, NEG_INF, gate)


def _gate(qm8, bsum):
    nb_s, n_blk = bsum.shape[:2]
    assert n_blk >= MOBA_TOPK
    return pl.pallas_call(
        functools.partial(_gate_kernel, n_blk=n_blk),
        grid=(nb_s,),
        in_specs=[pl.BlockSpec((None, N_HEADS, HEAD_DIM), lambda b: (b, 0, 0)),
                  pl.BlockSpec((None, n_blk, N_HEADS, HEAD_DIM), lambda b: (b, 0, 0, 0))],
        out_specs=pl.BlockSpec((None, MOBA_TOPK, N_HEADS, 1), lambda b: (b, 0, 0, 0)),
        out_shape=jax.ShapeDtypeStruct((nb_s, MOBA_TOPK, N_HEADS, 1), jnp.int32),
        compiler_params=_params("arbitrary"),
        name="gate",
    )(qm8, bsum)


def _msel_kernel(sel_ref, pt_ref, slopes_ref, *refs, past_len):
    ppb = MOBA_BLOCK // PAGE_SIZE
    n_pg = MOBA_TOPK * ppb
    k_refs = refs[0:n_pg]
    v_refs = refs[n_pg:2 * n_pg]
    q_ref, kn_ref, vn_ref, g_ref, o_ref = refs[2 * n_pg:]
    b = pl.program_id(0)
    h = pl.program_id(1)
    slope = slopes_ref[h]
    q = q_ref[...]
    flat = PAGE_SIZE * N_HEADS
    k = jnp.concatenate([r[...].reshape(flat, HEAD_DIM) for r in k_refs], axis=0).astype(BF16)
    v = jnp.concatenate([r[...].reshape(flat, HEAD_DIM) for r in v_refs], axis=0).astype(BF16)
    q8 = jnp.broadcast_to(q, (8, HEAD_DIM)).astype(BF16)
    j_iota = lax.broadcasted_iota(jnp.int32, (1, flat), 1)
    key_in_page = j_iota // N_HEADS
    dist = jnp.concatenate(
        [past_len - (sel_ref[b, kk * N_HEADS + h] * MOBA_BLOCK + jj * PAGE_SIZE + key_in_page)
         for kk in range(MOBA_TOPK) for jj in range(ppb)], axis=1)
    own_head = jnp.concatenate([j_iota % N_HEADS == h] * n_pg, axis=1)
    s = jnp.where(own_head, _dot_nt(q8, k)[0:1] - slope * dist.astype(F32), NEG_INF)
    s_self = jnp.sum(q * kn_ref[...], axis=-1, keepdims=True)
    m = jnp.maximum(jnp.max(s, axis=-1, keepdims=True), s_self)
    p = jnp.exp(s - m)
    p_self = jnp.exp(s_self - m)
    denom = jnp.sum(p, axis=-1, keepdims=True) + p_self
    p8 = jnp.broadcast_to(p, (8, p.shape[1])).astype(BF16)
    o = (_dot(p8, v)[0:1] + p_self * vn_ref[...]) / denom
    o_ref[...] = _head_rms(o, g_ref[...]).astype(BF16)


def _msel(sel, page_table, slopes, ckm, cvm, q_s, kn_s, vn_s, out_g, past_len):
    nb_s = page_table.shape[0]
    ppb = MOBA_BLOCK // PAGE_SIZE

    def page_spec(kk, jj):
        return pl.BlockSpec((None, None, PAGE_SIZE, N_HEADS, HEAD_DIM),
                            lambda b, h, sel, pt: (0, pt[b, sel[b, kk * N_HEADS + h] * ppb + jj], 0, 0, 0))

    pages = [page_spec(kk, jj) for kk in range(MOBA_TOPK) for jj in range(ppb)]
    head_spec = pl.BlockSpec((None, None, 1, HEAD_DIM), lambda b, h, sel, pt: (b, h, 0, 0))
    return pl.pallas_call(
        functools.partial(_msel_kernel, past_len=past_len),
        grid_spec=pltpu.PrefetchScalarGridSpec(
            num_scalar_prefetch=2,
            grid=(nb_s, N_HEADS),
            in_specs=[pl.BlockSpec(memory_space=pltpu.SMEM)] + pages + pages
            + [head_spec, head_spec, head_spec,
               pl.BlockSpec((1, HEAD_DIM), lambda b, h, sel, pt: (0, 0))],
            out_specs=head_spec,
        ),
        out_shape=jax.ShapeDtypeStruct((nb_s, N_HEADS, 1, HEAD_DIM), BF16),
        compiler_params=_params("arbitrary", "arbitrary"),
        name="msel",
    )(sel, page_table, slopes, *([ckm] * len(pages)), *([cvm] * len(pages)), q_s, kn_s, vn_s, out_g)


def _oproj_kernel(od_ref, om_ref, x_ref, wa_ref, wb_ref, g_ref, h_ref, hn_ref):
    hval = x_ref[...] + _dot(od_ref[...], wa_ref[...]) + _dot(om_ref[...], wb_ref[...])
    h_ref[...] = hval
    ms = jnp.mean(hval * hval, axis=-1, keepdims=True)
    hn_ref[...] = (hval * lax.rsqrt(ms + EPS) * g_ref[...]).astype(BF16)


def _oproj(od, om, x, w_out_bf, mlp_g, tm):
    rows = x.shape[0]
    return pl.pallas_call(
        _oproj_kernel,
        grid=(rows // tm,),
        in_specs=[
            pl.BlockSpec((tm, D_GROUP), lambda i: (i, 0)),
            pl.BlockSpec((tm, D_GROUP), lambda i: (i, 0)),
            pl.BlockSpec((tm, D_MODEL), lambda i: (i, 0)),
            pl.BlockSpec((D_GROUP, D_MODEL), lambda i: (0, 0)),
            pl.BlockSpec((D_GROUP, D_MODEL), lambda i: (1, 0)),
            pl.BlockSpec((1, D_MODEL), lambda i: (0, 0)),
        ],
        out_specs=[pl.BlockSpec((tm, D_MODEL), lambda i: (i, 0)),
                   pl.BlockSpec((tm, D_MODEL), lambda i: (i, 0))],
        out_shape=[jax.ShapeDtypeStruct((rows, D_MODEL), F32),
                   jax.ShapeDtypeStruct((rows, D_MODEL), BF16)],
        compiler_params=_params("arbitrary"),
        name="oproj",
    )(od, om, x, w_out_bf, w_out_bf, mlp_g)


def _mlp_kernel(hn_ref, h_ref, wu_ref, wd_ref, y_ref):
    f = pl.program_id(1)
    u = jnp.maximum(_dot(hn_ref[...], wu_ref[...]), 0.0)
    contrib = _dot((u * u).astype(BF16), wd_ref[...])

    @pl.when(f == 0)
    def _():
        y_ref[...] = h_ref[...] + contrib

    @pl.when(f > 0)
    def _():
        y_ref[...] += contrib


def _mlp(hn, h, w_up_bf, w_down_bf, tm, tf):
    rows = h.shape[0]
    return pl.pallas_call(
        _mlp_kernel,
        grid=(rows // tm, D_FF // tf),
        in_specs=[
            pl.BlockSpec((tm, D_MODEL), lambda i, f: (i, 0)),
            pl.BlockSpec((tm, D_MODEL), lambda i, f: (i, 0)),
            pl.BlockSpec((D_MODEL, tf), lambda i, f: (0, f)),
            pl.BlockSpec((tf, D_MODEL), lambda i, f: (f, 0)),
        ],
        out_specs=pl.BlockSpec((tm, D_MODEL), lambda i, f: (i, 0)),
        out_shape=jax.ShapeDtypeStruct((rows, D_MODEL), F32),
        compiler_params=_params("arbitrary", "arbitrary"),
        name="mlp",
    )(hn, h, w_up_bf, w_down_bf)


def _alibi_slopes(n):
    return jnp.exp2(-8.0 * jnp.arange(1, n + 1, dtype=F32) / n)


def kernel(x_prompt, x_sample, cache_k_diff, cache_v_diff, cache_k_moba, cache_v_moba, page_table,
           attn_norm_g, w_in, qn_diff_g, kn_diff_g, qn_moba_g, kn_moba_g,
           lambda_q1, lambda_k1, lambda_q2, lambda_k2, subln_diff_g, out_norm_moba_g, w_out,
           mlp_norm_g, w_up, w_down):
    bp, sp, _ = x_prompt.shape
    bs, ss, _ = x_sample.shape
    assert ss == 1 and w_in.shape[0] == 1
    n_pool = cache_k_diff.shape[1]
    past_len = page_table.shape[1] * PAGE_SIZE
    assert past_len % MOBA_BLOCK == 0
    layer = 0

    slopes = _alibi_slopes(N_HEADS)
    w_in_bf = w_in[layer].astype(BF16)
    w_out_bf = w_out[layer].astype(BF16)
    w_up_bf = w_up[layer].astype(BF16)
    w_down_bf = w_down[layer].astype(BF16)
    attn_g = attn_norm_g[layer].reshape(1, D_MODEL)
    mlp_g = mlp_norm_g[layer].reshape(1, D_MODEL)
    subln_g = subln_diff_g[layer].reshape(1, HEAD_DIM)
    out_g = out_norm_moba_g[layer].reshape(1, HEAD_DIM)
    lam4 = jnp.stack([lambda_q1[layer], lambda_k1[layer], lambda_q2[layer], lambda_k2[layer]]).astype(F32)

    ones = jnp.ones((HEAD_DIM,), F32)
    qd_g = jnp.tile(qn_diff_g[layer], 2)
    kd_g = jnp.tile(kn_diff_g[layer], 2)
    gains = jnp.stack([qd_g, kd_g, ones, qn_moba_g[layer], kn_moba_g[layer], ones]).reshape(6, 1, HEAD_DIM)
    scales = jnp.array([DQK_DIFF ** -0.5, 1.0, 1.0, HEAD_DIM ** -0.5, 1.0, 1.0], F32).reshape(6, 1, 1)
    gains_b = gains * scales
    lane = jnp.arange(HEAD_DIM)
    bm_diff = (lane[:, None] // DQK_DIFF == lane[None, :] // DQK_DIFF).astype(F32) / DQK_DIFF
    bm_moba = jnp.full((HEAD_DIM, HEAD_DIM), 1.0 / HEAD_DIM, F32)
    bmats = jnp.stack([bm_diff, bm_moba]).astype(BF16)

    def dense_tail(od, om, x2d, tm, tf):
        h, hn = _oproj(od, om, x2d, w_out_bf, mlp_g, tm)
        return _mlp(hn, h, w_up_bf, w_down_bf, tm, tf)

    xp2 = x_prompt.reshape(bp * sp, D_MODEL)
    kd_p, vd_p, qm_p, km_p, vm_p, qkv_p = _proj(xp2, attn_g, w_in_bf, gains, gains_b, bmats, tm=512)
    kmean_p = _kmean(km_p)
    od_p = _diff_attn(slopes, qkv_p, lam4, subln_g, bp, sp)
    om_p = _moba_attn(slopes, qkv_p, qm_p, kmean_p, out_g, bp, sp)
    y_prompt = dense_tail(od_p, om_p, xp2, 512, 1024).reshape(bp, sp, D_MODEL)

    xs2 = x_sample.reshape(bs, D_MODEL)
    kd_s, vd_s, qm_s, km_s, vm_s, qkv_s = _proj(xs2, attn_g, w_in_bf, gains, gains_b, bmats, tm=bs)
    heads = lambda a: a.reshape(bs, N_HEADS, HEAD_DIM)
    twice = lambda a: jnp.concatenate([heads(a), heads(a)], axis=1)
    heads4 = lambda a: a.reshape(bs, N_HEADS, 1, HEAD_DIM)
    qd_s = qkv_s[:, :D_GROUP].astype(F32)
    qms_scaled = qkv_s[:, 3 * D_GROUP:4 * D_GROUP].astype(F32)
    od_s, bsum = _sweep(page_table, slopes, cache_k_diff, cache_v_diff, cache_k_moba,
                        twice(qd_s), twice(kd_s), twice(vd_s), lam4, subln_g, past_len)
    sel = _gate(heads(qm_s), bsum).reshape(bs, MOBA_TOPK * N_HEADS)
    om_s = _msel(sel, page_table, slopes, cache_k_moba, cache_v_moba,
                 heads4(qms_scaled), heads4(km_s), heads4(vm_s), out_g, past_len)
    y_sample = dense_tail(od_s.reshape(bs, D_GROUP), om_s.reshape(bs, D_GROUP), xs2, bs, 1024)
    y_sample = y_sample.reshape(bs, ss, D_MODEL)

    kv_p = lambda a: a.reshape(1, bp, sp, N_HEADS, HEAD_DIM)
    kv_s = lambda a: a.reshape(1, bs, ss, N_HEADS, HEAD_DIM)
    return (y_prompt, y_sample, kv_p(kd_p), kv_p(vd_p), kv_p(km_p), kv_p(vm_p),
            kv_s(kd_s), kv_s(vd_s), kv_s(km_s), kv_s(vm_s))
```

```python
import functools
import math

import jax
import jax.numpy as jnp
from jax import lax
from jax.experimental import pallas as pl
from jax.experimental.pallas import tpu as pltpu

F32 = jnp.float32
BF16 = jnp.bfloat16

D_MODEL = 2048
HEAD_DIM = 128
N_HEADS = 8
D_GROUP = N_HEADS * HEAD_DIM
DQK_DIFF = HEAD_DIM // 2
MOBA_BLOCK = 256
MOBA_TOPK = 3
PAGE_SIZE = 128
D_FF = 4 * D_MODEL
EPS = 1e-6
NEG_INF = float("-inf")
LAM_INIT = 0.8 - 0.6 * math.exp(-0.3 * 0)
LOG2E = math.log2(math.e)

VMEM_LIMIT = 56 * 1024 * 1024


def _dot(a, b):
    return jnp.dot(a, b, preferred_element_type=F32)


def _dot_nt(a, b):
    return lax.dot_general(a, b, (((1,), (1,)), ((), ())), preferred_element_type=F32)


def _params(*sem):
    return pltpu.CompilerParams(dimension_semantics=sem, vmem_limit_bytes=VMEM_LIMIT)


def _proj_kernel(x_ref, g_ref, w_ref, gain_ref, gainb_ref, bmat_ref,
                 kd_ref, vd_ref, qm_ref, km_ref, vm_ref, bf_ref, xn_ref):
    j = pl.program_id(1)

    @pl.when(j == 0)
    def _():
        x = x_ref[...]
        ms = jnp.mean(x * x, axis=-1, keepdims=True)
        xn_ref[...] = (x * lax.rsqrt(ms + EPS) * g_ref[...]).astype(BF16)

    z = _dot(xn_ref[...], w_ref[...])
    is_v = (j % 3) == 2
    bmat = bmat_ref[0]
    ys, ybs = [], []
    for hh in range(N_HEADS):
        zc = z[:, hh * HEAD_DIM:(hh + 1) * HEAD_DIM]
        ms = _dot((zc * zc).astype(BF16), bmat)
        rs = jnp.where(is_v, 1.0, lax.rsqrt(ms + EPS))
        zr = zc * rs
        ys.append(zr * gain_ref[0])
        ybs.append((zr * gainb_ref[0]).astype(BF16))
    y = jnp.concatenate(ys, axis=1)
    bf_ref[...] = jnp.concatenate(ybs, axis=1)

    for jj, ref in ((1, kd_ref), (2, vd_ref), (3, qm_ref), (4, km_ref), (5, vm_ref)):
        @pl.when(j == jj)
        def _(ref=ref):
            ref[...] = y


def _proj(x, g, w_bf, gains, gains_b, bmats, tm):
    rows = x.shape[0]
    n_i = rows // tm
    f32_out = jax.ShapeDtypeStruct((rows, D_GROUP), F32)
    f32_spec = pl.BlockSpec((tm, D_GROUP), lambda i, j: (i, 0))
    return pl.pallas_call(
        _proj_kernel,
        grid=(n_i, 6),
        in_specs=[
            pl.BlockSpec((tm, D_MODEL), lambda i, j: (i, 0)),
            pl.BlockSpec((1, D_MODEL), lambda i, j: (0, 0)),
            pl.BlockSpec((D_MODEL, D_GROUP), lambda i, j: (0, j)),
            pl.BlockSpec((1, 1, HEAD_DIM), lambda i, j: (j, 0, 0)),
            pl.BlockSpec((1, 1, HEAD_DIM), lambda i, j: (j, 0, 0)),
            pl.BlockSpec((1, HEAD_DIM, HEAD_DIM), lambda i, j: (j // 3, 0, 0)),
        ],
        out_specs=[f32_spec] * 5 + [pl.BlockSpec((tm, D_GROUP), lambda i, j: (i, j))],
        out_shape=[f32_out] * 5 + [jax.ShapeDtypeStruct((rows, 6 * D_GROUP), BF16)],
        scratch_shapes=[pltpu.VMEM((tm, D_MODEL), BF16)],
        compiler_params=_params("arbitrary", "arbitrary"),
        name="proj",
    )(x, g, w_bf, gains, gains_b, bmats)


def _kmean_kernel(k_ref, o_ref, *, nb):
    k = k_ref[...].reshape(nb, MOBA_BLOCK, D_GROUP)
    o_ref[...] = jnp.sum(k, axis=1) * (1.0 / MOBA_BLOCK)


def _kmean(km, nb=8):
    rows = km.shape[0]
    return pl.pallas_call(
        functools.partial(_kmean_kernel, nb=nb),
        grid=(rows // (nb * MOBA_BLOCK),),
        in_specs=[pl.BlockSpec((nb * MOBA_BLOCK, D_GROUP), lambda i: (i, 0))],
        out_specs=pl.BlockSpec((nb, D_GROUP), lambda i: (i, 0)),
        out_shape=jax.ShapeDtypeStruct((rows // MOBA_BLOCK, D_GROUP), F32),
        compiler_params=_params("arbitrary"),
        name="kmean",
    )(km)


def _lam(lam_ref):
    l = lam_ref[...]
    a = jnp.sum(l[0:1] * l[1:2], axis=-1, keepdims=True)
    b = jnp.sum(l[2:3] * l[3:4], axis=-1, keepdims=True)
    return jnp.exp(a) - jnp.exp(b) + LAM_INIT


def _head_rms(o, gain):
    return o * lax.rsqrt(jnp.mean(o * o, axis=-1, keepdims=True) + EPS) * gain


def _diff_body(h, qi, slopes_ref, q_ref, k_ref, v_ref, lam_ref, g_ref, o_ref, *, tq, tk):
    slope = slopes_ref[h]
    q = q_ref[...]
    lane = lax.broadcasted_iota(jnp.int32, (1, HEAD_DIM), 1)
    zero = jnp.zeros_like(q)
    q1 = jnp.where(lane < DQK_DIFF, q, zero)
    q2 = jnp.where(lane >= DQK_DIFF, q, zero)
    q_start = qi * tq
    col = lax.broadcasted_iota(jnp.int32, (1, tk), 1)

    def tile(kt, carry, masked):
        m1, l1, a1, m2, l2, a2 = carry
        k_start = pl.multiple_of(kt * tk, tk)
        k = k_ref[pl.ds(k_start, tk), :]
        v = v_ref[pl.ds(k_start, tk), :]
        bias = slope * (col + (k_start - q_start)).astype(F32)
        if masked:
            row = lax.broadcasted_iota(jnp.int32, (tq, tk), 0)
            ok = (row + q_start) >= (col + k_start)

        def one(qc, m, l, a):
            s = _dot_nt(qc, k) + bias
            if masked:
                s = jnp.where(ok, s, NEG_INF)
            m_new = jnp.maximum(m, jnp.max(s, axis=-1, keepdims=True))
            alpha = jnp.exp(m - m_new)
            p = jnp.exp(s - m_new)
            l = alpha * l + jnp.sum(p, axis=-1, keepdims=True)
            a = alpha * a + _dot(p.astype(BF16), v)
            return m_new, l, a

        m1, l1, a1 = one(q1, m1, l1, a1)
        m2, l2, a2 = one(q2, m2, l2, a2)
        return m1, l1, a1, m2, l2, a2

    m0 = jnp.full((tq, 1), NEG_INF, F32)
    l0 = jnp.zeros((tq, 1), F32)
    a0 = jnp.zeros((tq, HEAD_DIM), F32)
    n_full = q_start // tk
    carry = lax.fori_loop(0, n_full, lambda kt, c: tile(kt, c, False), (m0, l0, a0, m0, l0, a0))
    m1, l1, a1, m2, l2, a2 = tile(n_full, carry, True)
    o = a1 / l1 - _lam(lam_ref) * (a2 / l2)
    o_ref[...] = (_head_rms(o, g_ref[...]) * (1.0 - LAM_INIT)).astype(BF16)


def _moba_kernel(slopes_ref, q_ref, qf_ref, k_ref, v_ref, kmean_ref, g_ref, o_ref, *, n_kb, hps, tq):
    hp = pl.program_id(1)
    i = pl.program_id(2)
    blk = MOBA_BLOCK
    hd = HEAD_DIM
    tk = tq
    bpt = tq // blk
    rowi = lax.broadcasted_iota(jnp.int32, (tq, 1), 0)
    coli = lax.broadcasted_iota(jnp.int32, (1, tk), 1)
    bidx_i = lax.broadcasted_iota(jnp.int32, (1, n_kb), 1)
    bidx = bidx_i.astype(F32)
    cq = i * bpt + rowi // blk

    qs, sels, slopes = [], [], []
    for e in range(hps):
        cs = slice(e * hd, (e + 1) * hd)
        gate = lax.dot_general(qf_ref[:, cs], kmean_ref[:, cs], (((1,), (1,)), ((), ())),
                               precision=lax.Precision.HIGHEST, preferred_element_type=F32)
        gate = jnp.where(bidx_i < cq, gate, NEG_INF)
        sel = jnp.zeros((tq, n_kb), F32)
        for _ in range(MOBA_TOPK):
            mx = jnp.max(gate, axis=-1, keepdims=True)
            first = jnp.min(jnp.where(gate == mx, bidx, float(n_kb)), axis=-1, keepdims=True)
            hit = (bidx == first) & (mx > NEG_INF)
            sel = jnp.where(hit, 1.0, sel)
            gate = jnp.where(hit, NEG_INF, gate)
        sels.append(sel)
        qs.append(q_ref[:, cs])
        slopes.append(slopes_ref[hp * hps + e])

    def sel_col(sel, n):
        return jnp.sum(jnp.where(bidx_i == n, sel, 0.0), axis=-1, keepdims=True)

    def tile(t, carry, last):
        k_start = pl.multiple_of(t * tk, tk)
        out = []
        for e in range(hps):
            cs = slice(e * hd, (e + 1) * hd)
            k = k_ref[pl.ds(k_start, tk), cs]
            v = v_ref[pl.ds(k_start, tk), cs]
            s = _dot_nt(qs[e], k) + slopes[e] * (coli + (t - i) * tk).astype(F32)
            if last:
                ok = coli <= rowi
                for hb in range(bpt - 1):
                    need_sel = (rowi // blk > hb) & (coli // blk == hb)
                    picked = sel_col(sels[e], t * bpt + hb) > 0.0
                    ok = ok & (jnp.logical_not(need_sel) | picked)
                s = jnp.where(ok, s, NEG_INF)
            else:
                s = jnp.concatenate(
                    [jnp.where(sel_col(sels[e], t * bpt + hb) > 0.0, s[:, hb * blk:(hb + 1) * blk], NEG_INF)
                     for hb in range(bpt)], axis=1)
            m, l, a = carry[e]
            m_new = jnp.maximum(m, jnp.max(s, axis=-1, keepdims=True))
            m_use = jnp.where(m_new == NEG_INF, 0.0, m_new)
            alpha = jnp.exp2(m - m_use)
            p = jnp.exp2(s - m_use)
            l = alpha * l + jnp.sum(p, axis=-1, keepdims=True)
            a = alpha * a + _dot(p.astype(BF16), v)
            out.append((m_new, l, a))
        return tuple(out)

    init = (jnp.full((tq, 1), NEG_INF, F32), jnp.zeros((tq, 1), F32), jnp.zeros((tq, hd), F32))
    carry = lax.fori_loop(0, i, lambda t, cr: tile(t, cr, False), (init,) * hps)
    carry = tile(i, carry, True)
    outs = []
    for e in range(hps):
        m, l, a = carry[e]
        outs.append(_head_rms(a / l, g_ref[...]))
    o_ref[...] = jnp.concatenate(outs, axis=1).astype(BF16)


def _moba_attn(slopes2, qkv_bf, qm_f32, kmean, out_g, batch, seq, hps=2, tq=512):
    n_kb = seq // MOBA_BLOCK
    nq = seq // tq
    w = hps * HEAD_DIM
    nhp = N_HEADS // hps
    return pl.pallas_call(
        functools.partial(_moba_kernel, n_kb=n_kb, hps=hps, tq=tq),
        grid=(batch, nhp, nq),
        in_specs=[
            pl.BlockSpec(memory_space=pltpu.SMEM),
            pl.BlockSpec((tq, w), lambda b, h, i: (b * nq + i, 3 * nhp + h)),
            pl.BlockSpec((tq, w), lambda b, h, i: (b * nq + i, h)),
            pl.BlockSpec((seq, w), lambda b, h, i: (b, 4 * nhp + h)),
            pl.BlockSpec((seq, w), lambda b, h, i: (b, 5 * nhp + h)),
            pl.BlockSpec((n_kb, w), lambda b, h, i: (b, h)),
            pl.BlockSpec((1, HEAD_DIM), lambda b, h, i: (0, 0)),
        ],
        out_specs=pl.BlockSpec((tq, w), lambda b, h, i: (b * nq + i, h)),
        out_shape=jax.ShapeDtypeStruct((batch * seq, D_GROUP), BF16),
        compiler_params=_params("arbitrary", "arbitrary", "arbitrary"),
        name="moba",
    )(slopes2, qkv_bf, qm_f32, qkv_bf, qkv_bf, kmean, out_g)


def _sweep_body(pp, n_pp, kd_refs, vd_refs, km_refs, q_ref, kn_ref, vn_ref, hmask_ref, tmpl_ref, slope_ref,
                lam_ref, g_ref, o_ref, bsum_ref, m_ref, l_ref, acc_ref, *, pps, past_len):
    nk = pps * PAGE_SIZE
    flat = PAGE_SIZE * N_HEADS

    qm = q_ref[...] * hmask_ref[...]

    @pl.when(pp == 0)
    def _():
        m_ref[...] = jnp.sum(qm * kn_ref[...], axis=-1, keepdims=True)
        l_ref[...] = jnp.ones_like(l_ref)
        acc_ref[...] = vn_ref[...]

    k = jnp.concatenate([r[...].reshape(flat, HEAD_DIM) for r in kd_refs], axis=0).astype(BF16)
    v = jnp.concatenate([r[...].reshape(flat, HEAD_DIM) for r in vd_refs], axis=0).astype(BF16)
    base = jnp.full((1, 1), pp * nk - past_len, jnp.int32).astype(F32)
    s = _dot_nt(qm.astype(BF16), k) + tmpl_ref[...] + slope_ref[...] * base
    m_old = m_ref[...]
    m_new = jnp.maximum(m_old, jnp.max(s, axis=-1, keepdims=True))
    alpha = jnp.exp(m_old - m_new)
    p = jnp.exp(s - m_new)
    l_ref[...] = alpha * l_ref[...] + jnp.sum(p, axis=-1, keepdims=True)
    acc_ref[...] = alpha * acc_ref[...] + _dot(p.astype(BF16), v)
    m_ref[...] = m_new

    ppb = MOBA_BLOCK // PAGE_SIZE
    sums = [jnp.sum(r[...], axis=0) for r in km_refs]
    for i in range(pps // ppb):
        bsum_ref[i] = sum(sums[i * ppb:(i + 1) * ppb])

    @pl.when(pp == n_pp - 1)
    def _():
        o16 = acc_ref[...] / l_ref[...]
        o = o16[0:N_HEADS] - _lam(lam_ref) * o16[N_HEADS:2 * N_HEADS]
        o_ref[...] = (_head_rms(o, g_ref[...]) * (1.0 - LAM_INIT)).astype(BF16)


def _diffsweep_kernel(pt_ref, slopes_ref, q_ref, k_ref, v_ref, lam_ref, g_ref, *refs,
                      tq, tk, nq, pps, n_pp, past_len):
    kd_refs = refs[0:pps]
    vd_refs = refs[pps:2 * pps]
    km_refs = refs[2 * pps:3 * pps]
    (qs_ref, kn_ref, vn_ref, hmask_ref, tmpl_ref, slope_ref,
     o_ref, os_ref, bsum_ref, m_ref, l_ref, acc_ref) = refs[3 * pps:]
    b = pl.program_id(0)
    h = pl.program_id(1)
    qi = pl.program_id(2)
    step = (b * N_HEADS + h) * nq + qi
    _sweep_body(step % n_pp, n_pp, kd_refs, vd_refs, km_refs, qs_ref, kn_ref, vn_ref, hmask_ref, tmpl_ref,
                slope_ref, lam_ref, g_ref, os_ref, bsum_ref, m_ref, l_ref, acc_ref, pps=pps, past_len=past_len)
    _diff_body(h, qi, slopes_ref, q_ref, k_ref, v_ref, lam_ref, g_ref, o_ref, tq=tq, tk=tk)


def _diffsweep(page_table, slopes, qkv_bf, ckd, cvd, ckm, q16, kn16, vn16, lam4, subln_g,
               batch, seq, past_len, tq=256, tk=1024):
    nq = seq // tq
    nb_s, n_pages = page_table.shape
    n_steps = batch * N_HEADS * nq
    pps = nb_s * n_pages // n_steps
    assert pps * n_steps == nb_s * n_pages and n_pages % pps == 0
    n_pp = n_pages // pps
    ppb = MOBA_BLOCK // PAGE_SIZE
    assert pps % ppb == 0
    bps = pps // ppb
    n_rows = 2 * N_HEADS
    n_cols = pps * PAGE_SIZE * N_HEADS

    row = jnp.arange(n_rows)[:, None]
    colj = jnp.arange(n_cols)[None, :]
    slope16 = slopes[row % N_HEADS]
    lane = jnp.arange(HEAD_DIM)[None, :]
    hmask = ((lane // DQK_DIFF) == (row // N_HEADS)).astype(F32)
    tmpl = jnp.where(colj % N_HEADS == row % N_HEADS,
                     slope16 * (colj // N_HEADS).astype(F32), NEG_INF)

    def step_of(b, h, i):
        return (b * N_HEADS + h) * nq + i

    def page_spec(r):
        def index(b, h, i, pt):
            st = step_of(b, h, i)
            return (0, pt[st // n_pp, (st % n_pp) * pps + r], 0, 0, 0)
        return pl.BlockSpec((None, None, PAGE_SIZE, N_HEADS, HEAD_DIM), index)

    def const_spec(shape):
        return pl.BlockSpec(shape, lambda b, h, i, pt: (0,) * len(shape))

    row_spec = pl.BlockSpec((None, n_rows, HEAD_DIM), lambda b, h, i, pt: (step_of(b, h, i) // n_pp, 0, 0))
    return pl.pallas_call(
        functools.partial(_diffsweep_kernel, tq=tq, tk=tk, nq=nq, pps=pps, n_pp=n_pp, past_len=past_len),
        grid_spec=pltpu.PrefetchScalarGridSpec(
            num_scalar_prefetch=1,
            grid=(batch, N_HEADS, nq),
            in_specs=[
                pl.BlockSpec(memory_space=pltpu.SMEM),
                pl.BlockSpec((tq, HEAD_DIM), lambda b, h, i, pt: (b * nq + i, h)),
                pl.BlockSpec((seq, HEAD_DIM), lambda b, h, i, pt: (b, N_HEADS + h)),
                pl.BlockSpec((seq, HEAD_DIM), lambda b, h, i, pt: (b, 2 * N_HEADS + h)),
                const_spec((4, DQK_DIFF)), const_spec((1, HEAD_DIM))]
            + [page_spec(r) for r in range(pps)] * 3
            + [row_spec, row_spec, row_spec,
               const_spec((n_rows, HEAD_DIM)), const_spec((n_rows, n_cols)), const_spec((n_rows, 1))],
            out_specs=[
                pl.BlockSpec((tq, HEAD_DIM), lambda b, h, i, pt: (b * nq + i, h)),
                pl.BlockSpec((None, N_HEADS, HEAD_DIM), lambda b, h, i, pt: (step_of(b, h, i) // n_pp, 0, 0)),
                pl.BlockSpec((None, bps, N_HEADS, HEAD_DIM),
                             lambda b, h, i, pt: (step_of(b, h, i) // n_pp, step_of(b, h, i) % n_pp, 0, 0))],
            scratch_shapes=[pltpu.VMEM((n_rows, 1), F32), pltpu.VMEM((n_rows, 1), F32),
                            pltpu.VMEM((n_rows, HEAD_DIM), F32)],
        ),
        out_shape=[jax.ShapeDtypeStruct((batch * seq, D_GROUP), BF16),
                   jax.ShapeDtypeStruct((nb_s, N_HEADS, HEAD_DIM), BF16),
                   jax.ShapeDtypeStruct((nb_s, n_pages // ppb, N_HEADS, HEAD_DIM), F32)],
        compiler_params=_params("arbitrary", "arbitrary", "arbitrary"),
        name="diffsweep",
    )(page_table, slopes, qkv_bf, qkv_bf, qkv_bf, lam4, subln_g,
      *([ckd] * pps), *([cvd] * pps), *([ckm] * pps), q16, kn16, vn16, hmask, tmpl, slope16)


def _gate_kernel(q_ref, bsum_ref, sel_ref, *, n_blk):
    kmean = bsum_ref[...] * (1.0 / MOBA_BLOCK)
    gate = jnp.sum(kmean * q_ref[...], axis=-1, keepdims=True)
    bidx = lax.broadcasted_iota(jnp.int32, gate.shape, 0).astype(F32)
    for r in range(MOBA_TOPK):
        mx = jnp.max(gate, axis=0, keepdims=True)
        first = jnp.min(jnp.where(gate == mx, bidx, float(n_blk)), axis=0, keepdims=True)
        sel_ref[r] = first[0].astype(jnp.int32)
        gate = jnp.where(bidx == first, NEG_INF, gate)


def _gate(qm8, bsum):
    nb_s, n_blk = bsum.shape[:2]
    assert n_blk >= MOBA_TOPK
    return pl.pallas_call(
        functools.partial(_gate_kernel, n_blk=n_blk),
        grid=(nb_s,),
        in_specs=[pl.BlockSpec((None, N_HEADS, HEAD_DIM), lambda b: (b, 0, 0)),
                  pl.BlockSpec((None, n_blk, N_HEADS, HEAD_DIM), lambda b: (b, 0, 0, 0))],
        out_specs=pl.BlockSpec((None, MOBA_TOPK, N_HEADS, 1), lambda b: (b, 0, 0, 0)),
        out_shape=jax.ShapeDtypeStruct((nb_s, MOBA_TOPK, N_HEADS, 1), jnp.int32),
        compiler_params=_params("arbitrary"),
        name="gate",
    )(qm8, bsum)


def _msel_kernel(sel_ref, pt_ref, slopes_ref, *refs, past_len):
    ppb = MOBA_BLOCK // PAGE_SIZE
    n_pg = MOBA_TOPK * ppb
    k_refs = refs[0:n_pg]
    v_refs = refs[n_pg:2 * n_pg]
    q_ref, kn_ref, vn_ref, g_ref, o_ref = refs[2 * n_pg:]
    b = pl.program_id(0)
    h = pl.program_id(1)
    slope = slopes_ref[h]
    q = q_ref[...]
    flat = PAGE_SIZE * N_HEADS
    k = jnp.concatenate([r[...].reshape(flat, HEAD_DIM) for r in k_refs], axis=0).astype(BF16)
    v = jnp.concatenate([r[...].reshape(flat, HEAD_DIM) for r in v_refs], axis=0).astype(BF16)
    q8 = jnp.broadcast_to(q, (8, HEAD_DIM)).astype(BF16)
    j_iota = lax.broadcasted_iota(jnp.int32, (1, flat), 1)
    key_in_page = j_iota // N_HEADS
    dist = jnp.concatenate(
        [past_len - (sel_ref[b, kk * N_HEADS + h] * MOBA_BLOCK + jj * PAGE_SIZE + key_in_page)
         for kk in range(MOBA_TOPK) for jj in range(ppb)], axis=1)
    own_head = jnp.concatenate([j_iota % N_HEADS == h] * n_pg, axis=1)
    s = jnp.where(own_head, _dot_nt(q8, k)[0:1] - slope * dist.astype(F32), NEG_INF)
    s_self = jnp.sum(q * kn_ref[...], axis=-1, keepdims=True)
    m = jnp.maximum(jnp.max(s, axis=-1, keepdims=True), s_self)
    p = jnp.exp2(s - m)
    p_self = jnp.exp2(s_self - m)
    denom = jnp.sum(p, axis=-1, keepdims=True) + p_self
    p8 = jnp.broadcast_to(p, (8, p.shape[1])).astype(BF16)
    o = (_dot(p8, v)[0:1] + p_self * vn_ref[...]) / denom
    o_ref[...] = _head_rms(o, g_ref[...]).astype(BF16)


def _msel(sel, page_table, slopes2, ckm, cvm, q_s, kn_s, vn_s, out_g, past_len):
    nb_s = page_table.shape[0]
    ppb = MOBA_BLOCK // PAGE_SIZE

    def page_spec(kk, jj):
        return pl.BlockSpec((None, None, PAGE_SIZE, N_HEADS, HEAD_DIM),
                            lambda b, h, sel, pt: (0, pt[b, sel[b, kk * N_HEADS + h] * ppb + jj], 0, 0, 0))

    pages = [page_spec(kk, jj) for kk in range(MOBA_TOPK) for jj in range(ppb)]
    head_spec = pl.BlockSpec((None, None, 1, HEAD_DIM), lambda b, h, sel, pt: (b, h, 0, 0))
    return pl.pallas_call(
        functools.partial(_msel_kernel, past_len=past_len),
        grid_spec=pltpu.PrefetchScalarGridSpec(
            num_scalar_prefetch=2,
            grid=(nb_s, N_HEADS),
            in_specs=[pl.BlockSpec(memory_space=pltpu.SMEM)] + pages + pages
            + [head_spec, head_spec, head_spec,
               pl.BlockSpec((1, HEAD_DIM), lambda b, h, sel, pt: (0, 0))],
            out_specs=head_spec,
        ),
        out_shape=jax.ShapeDtypeStruct((nb_s, N_HEADS, 1, HEAD_DIM), BF16),
        compiler_params=_params("arbitrary", "arbitrary"),
        name="msel",
    )(sel, page_table, slopes2, *([ckm] * len(pages)), *([cvm] * len(pages)), q_s, kn_s, vn_s, out_g)


def _oproj_kernel(od_ref, om_ref, x_ref, wa_ref, wb_ref, g_ref, h_ref, hn_ref):
    hval = x_ref[...] + _dot(od_ref[...], wa_ref[...]) + _dot(om_ref[...], wb_ref[...])
    h_ref[...] = hval
    ms = jnp.mean(hval * hval, axis=-1, keepdims=True)
    hn_ref[...] = (hval * lax.rsqrt(ms + EPS) * g_ref[...]).astype(BF16)


def _oproj(od, om, x, w_out_bf, mlp_g, tm):
    rows = x.shape[0]
    return pl.pallas_call(
        _oproj_kernel,
        grid=(rows // tm,),
        in_specs=[
            pl.BlockSpec((tm, D_GROUP), lambda i: (i, 0)),
            pl.BlockSpec((tm, D_GROUP), lambda i: (i, 0)),
            pl.BlockSpec((tm, D_MODEL), lambda i: (i, 0)),
            pl.BlockSpec((D_GROUP, D_MODEL), lambda i: (0, 0)),
            pl.BlockSpec((D_GROUP, D_MODEL), lambda i: (1, 0)),
            pl.BlockSpec((1, D_MODEL), lambda i: (0, 0)),
        ],
        out_specs=[pl.BlockSpec((tm, D_MODEL), lambda i: (i, 0)),
                   pl.BlockSpec((tm, D_MODEL), lambda i: (i, 0))],
        out_shape=[jax.ShapeDtypeStruct((rows, D_MODEL), F32),
                   jax.ShapeDtypeStruct((rows, D_MODEL), BF16)],
        compiler_params=_params("arbitrary"),
        name="oproj",
    )(od, om, x, w_out_bf, w_out_bf, mlp_g)


def _mlp_kernel(hn_ref, h_ref, wu_ref, wd_ref, y_ref):
    f = pl.program_id(1)
    u = jnp.maximum(_dot(hn_ref[...], wu_ref[...]), 0.0)
    contrib = _dot((u * u).astype(BF16), wd_ref[...])

    @pl.when(f == 0)
    def _():
        y_ref[...] = h_ref[...] + contrib

    @pl.when(f > 0)
    def _():
        y_ref[...] += contrib


def _mlp(hn, h, w_up_bf, w_down_bf, tm, tf):
    rows = h.shape[0]
    return pl.pallas_call(
        _mlp_kernel,
        grid=(rows // tm, D_FF // tf),
        in_specs=[
            pl.BlockSpec((tm, D_MODEL), lambda i, f: (i, 0)),
            pl.BlockSpec((tm, D_MODEL), lambda i, f: (i, 0)),
            pl.BlockSpec((D_MODEL, tf), lambda i, f: (0, f)),
            pl.BlockSpec((tf, D_MODEL), lambda i, f: (f, 0)),
        ],
        out_specs=pl.BlockSpec((tm, D_MODEL), lambda i, f: (i, 0)),
        out_shape=jax.ShapeDtypeStruct((rows, D_MODEL), F32),
        compiler_params=_params("arbitrary", "arbitrary"),
        name="mlp",
    )(hn, h, w_up_bf, w_down_bf)


def _alibi_slopes(n):
    return jnp.exp2(-8.0 * jnp.arange(1, n + 1, dtype=F32) / n)


def kernel(x_prompt, x_sample, cache_k_diff, cache_v_diff, cache_k_moba, cache_v_moba, page_table,
           attn_norm_g, w_in, qn_diff_g, kn_diff_g, qn_moba_g, kn_moba_g,
           lambda_q1, lambda_k1, lambda_q2, lambda_k2, subln_diff_g, out_norm_moba_g, w_out,
           mlp_norm_g, w_up, w_down):
    bp, sp, _ = x_prompt.shape
    bs, ss, _ = x_sample.shape
    assert ss == 1 and w_in.shape[0] == 1
    past_len = page_table.shape[1] * PAGE_SIZE
    assert past_len % MOBA_BLOCK == 0
    layer = 0

    slopes = _alibi_slopes(N_HEADS)
    slopes2 = slopes * LOG2E
    w_in_bf = w_in[layer].astype(BF16)
    w_out_bf = w_out[layer].astype(BF16)
    w_up_bf = w_up[layer].astype(BF16)
    w_down_bf = w_down[layer].astype(BF16)
    attn_g = attn_norm_g[layer].reshape(1, D_MODEL)
    mlp_g = mlp_norm_g[layer].reshape(1, D_MODEL)
    subln_g = subln_diff_g[layer].reshape(1, HEAD_DIM)
    out_g = out_norm_moba_g[layer].reshape(1, HEAD_DIM)
    lam4 = jnp.stack([lambda_q1[layer], lambda_k1[layer], lambda_q2[layer], lambda_k2[layer]]).astype(F32)

    ones = jnp.ones((HEAD_DIM,), F32)
    qd_g = jnp.tile(qn_diff_g[layer], 2)
    kd_g = jnp.tile(kn_diff_g[layer], 2)
    gains = jnp.stack([qd_g, kd_g, ones, qn_moba_g[layer], kn_moba_g[layer], ones]).reshape(6, 1, HEAD_DIM)
    scales = jnp.array([DQK_DIFF ** -0.5, 1.0, 1.0, HEAD_DIM ** -0.5 * LOG2E, 1.0, 1.0], F32).reshape(6, 1, 1)
    gains_b = gains * scales
    lane = jnp.arange(HEAD_DIM)
    bm_diff = (lane[:, None] // DQK_DIFF == lane[None, :] // DQK_DIFF).astype(F32) / DQK_DIFF
    bm_moba = jnp.full((HEAD_DIM, HEAD_DIM), 1.0 / HEAD_DIM, F32)
    bmats = jnp.stack([bm_diff, bm_moba]).astype(BF16)

    def dense_tail(od, om, x2d, tm, tf):
        h, hn = _oproj(od, om, x2d, w_out_bf, mlp_g, tm)
        return _mlp(hn, h, w_up_bf, w_down_bf, tm, tf)

    xp2 = x_prompt.reshape(bp * sp, D_MODEL)
    kd_p, vd_p, qm_p, km_p, vm_p, qkv_p = _proj(xp2, attn_g, w_in_bf, gains, gains_b, bmats, tm=512)
    xs2 = x_sample.reshape(bs, D_MODEL)
    kd_s, vd_s, qm_s, km_s, vm_s, qkv_s = _proj(xs2, attn_g, w_in_bf, gains, gains_b, bmats, tm=bs)
    heads = lambda a: a.reshape(bs, N_HEADS, HEAD_DIM)
    twice = lambda a: jnp.concatenate([heads(a), heads(a)], axis=1)
    heads4 = lambda a: a.reshape(bs, N_HEADS, 1, HEAD_DIM)
    qd_s = qkv_s[:, :D_GROUP].astype(F32)
    qms_scaled = qkv_s[:, 3 * D_GROUP:4 * D_GROUP].astype(F32)

    od_p, od_s, bsum = _diffsweep(page_table, slopes, qkv_p, cache_k_diff, cache_v_diff, cache_k_moba,
                                  twice(qd_s), twice(kd_s), twice(vd_s), lam4, subln_g, bp, sp, past_len)

    kmean_p = _kmean(km_p)
    om_p = _moba_attn(slopes2, qkv_p, qm_p, kmean_p, out_g, bp, sp)
    y_prompt = dense_tail(od_p, om_p, xp2, 512, 1024).reshape(bp, sp, D_MODEL)

    sel = _gate(heads(qm_s), bsum).reshape(bs, MOBA_TOPK * N_HEADS)
    om_s = _msel(sel, page_table, slopes2, cache_k_moba, cache_v_moba,
                 heads4(qms_scaled), heads4(km_s), heads4(vm_s), out_g, past_len)
    y_sample = dense_tail(od_s.reshape(bs, D_GROUP), om_s.reshape(bs, D_GROUP), xs2, bs, 1024)
    y_sample = y_sample.reshape(bs, ss, D_MODEL)

    kv_p = lambda a: a.reshape(1, bp, sp, N_HEADS, HEAD_DIM)
    kv_s = lambda a: a.reshape(1, bs, ss, N_HEADS, HEAD_DIM)
    return (y_prompt, y_sample, kv_p(kd_p), kv_p(vd_p), kv_p(km_p), kv_p(vm_p),
            kv_s(kd_s), kv_s(vd_s), kv_s(km_s), kv_s(vm_s))
```

```python
import functools
import math

import jax
import jax.numpy as jnp
from jax import lax
from jax.experimental import pallas as pl
from jax.experimental.pallas import tpu as pltpu

F32 = jnp.float32
BF16 = jnp.bfloat16

D_MODEL = 2048
HEAD_DIM = 128
N_HEADS = 8
D_GROUP = N_HEADS * HEAD_DIM
DQK_DIFF = HEAD_DIM // 2
MOBA_BLOCK = 256
MOBA_TOPK = 3
PAGE_SIZE = 128
D_FF = 4 * D_MODEL
EPS = 1e-6
NEG_INF = float("-inf")
LAM_INIT = 0.8 - 0.6 * math.exp(-0.3 * 0)
LOG2E = math.log2(math.e)

VMEM_LIMIT = 56 * 1024 * 1024


def _dot(a, b):
    return jnp.dot(a, b, preferred_element_type=F32)


def _dot_nt(a, b):
    return lax.dot_general(a, b, (((1,), (1,)), ((), ())), preferred_element_type=F32)


def _params(*sem):
    return pltpu.CompilerParams(dimension_semantics=sem, vmem_limit_bytes=VMEM_LIMIT)


def _proj_kernel(x_ref, g_ref, w_ref, gain_ref, gainb_ref, bmat_ref,
                 kd_ref, vd_ref, qm_ref, km_ref, vm_ref, bf_ref, xn_ref):
    j = pl.program_id(1)

    @pl.when(j == 0)
    def _():
        x = x_ref[...]
        ms = jnp.mean(x * x, axis=-1, keepdims=True)
        xn_ref[...] = (x * lax.rsqrt(ms + EPS) * g_ref[...]).astype(BF16)

    z = _dot(xn_ref[...], w_ref[...])
    is_v = (j % 3) == 2
    bmat = bmat_ref[0]
    ys, ybs = [], []
    for hh in range(N_HEADS):
        zc = z[:, hh * HEAD_DIM:(hh + 1) * HEAD_DIM]
        ms = _dot((zc * zc).astype(BF16), bmat)
        rs = jnp.where(is_v, 1.0, lax.rsqrt(ms + EPS))
        zr = zc * rs
        ys.append(zr * gain_ref[0])
        ybs.append((zr * gainb_ref[0]).astype(BF16))
    y = jnp.concatenate(ys, axis=1)
    bf_ref[...] = jnp.concatenate(ybs, axis=1)

    for jj, ref in ((1, kd_ref), (2, vd_ref), (3, qm_ref), (4, km_ref), (5, vm_ref)):
        @pl.when(j == jj)
        def _(ref=ref):
            ref[...] = y


def _proj(x, g, w_bf, gains, gains_b, bmats, tm):
    rows = x.shape[0]
    n_i = rows // tm
    f32_out = jax.ShapeDtypeStruct((rows, D_GROUP), F32)
    f32_spec = pl.BlockSpec((tm, D_GROUP), lambda i, j: (i, 0))
    return pl.pallas_call(
        _proj_kernel,
        grid=(n_i, 6),
        in_specs=[
            pl.BlockSpec((tm, D_MODEL), lambda i, j: (i, 0)),
            pl.BlockSpec((1, D_MODEL), lambda i, j: (0, 0)),
            pl.BlockSpec((D_MODEL, D_GROUP), lambda i, j: (0, j)),
            pl.BlockSpec((1, 1, HEAD_DIM), lambda i, j: (j, 0, 0)),
            pl.BlockSpec((1, 1, HEAD_DIM), lambda i, j: (j, 0, 0)),
            pl.BlockSpec((1, HEAD_DIM, HEAD_DIM), lambda i, j: (j // 3, 0, 0)),
        ],
        out_specs=[f32_spec] * 5 + [pl.BlockSpec((tm, D_GROUP), lambda i, j: (i, j))],
        out_shape=[f32_out] * 5 + [jax.ShapeDtypeStruct((rows, 6 * D_GROUP), BF16)],
        scratch_shapes=[pltpu.VMEM((tm, D_MODEL), BF16)],
        compiler_params=_params("arbitrary", "arbitrary"),
        name="proj",
    )(x, g, w_bf, gains, gains_b, bmats)


def _kmean_kernel(k_ref, o_ref, *, nb):
    k = k_ref[...].reshape(nb, MOBA_BLOCK, D_GROUP)
    o_ref[...] = jnp.sum(k, axis=1) * (1.0 / MOBA_BLOCK)


def _kmean(km, nb=8):
    rows = km.shape[0]
    return pl.pallas_call(
        functools.partial(_kmean_kernel, nb=nb),
        grid=(rows // (nb * MOBA_BLOCK),),
        in_specs=[pl.BlockSpec((nb * MOBA_BLOCK, D_GROUP), lambda i: (i, 0))],
        out_specs=pl.BlockSpec((nb, D_GROUP), lambda i: (i, 0)),
        out_shape=jax.ShapeDtypeStruct((rows // MOBA_BLOCK, D_GROUP), F32),
        compiler_params=_params("arbitrary"),
        name="kmean",
    )(km)


def _lam(lam_ref):
    l = lam_ref[...]
    a = jnp.sum(l[0:1] * l[1:2], axis=-1, keepdims=True)
    b = jnp.sum(l[2:3] * l[3:4], axis=-1, keepdims=True)
    return jnp.exp(a) - jnp.exp(b) + LAM_INIT


def _head_rms(o, gain):
    return o * lax.rsqrt(jnp.mean(o * o, axis=-1, keepdims=True) + EPS) * gain


def _diff_body(h, qi, slopes_ref, q_ref, k_ref, v_ref, lam_ref, g_ref, o_ref, *, tq, tk):
    slope = slopes_ref[h]
    q = q_ref[...]
    lane = lax.broadcasted_iota(jnp.int32, (1, HEAD_DIM), 1)
    zero = jnp.zeros_like(q)
    q1 = jnp.where(lane < DQK_DIFF, q, zero)
    q2 = jnp.where(lane >= DQK_DIFF, q, zero)
    q_start = qi * tq
    col = lax.broadcasted_iota(jnp.int32, (1, tk), 1)

    def tile(kt, carry, masked):
        m1, l1, a1, m2, l2, a2 = carry
        k_start = pl.multiple_of(kt * tk, tk)
        k = k_ref[pl.ds(k_start, tk), :]
        v = v_ref[pl.ds(k_start, tk), :]
        bias = slope * (col + (k_start - q_start)).astype(F32)
        if masked:
            row = lax.broadcasted_iota(jnp.int32, (tq, tk), 0)
            ok = (row + q_start) >= (col + k_start)

        def one(qc, m, l, a):
            s = _dot_nt(qc, k) + bias
            if masked:
                s = jnp.where(ok, s, NEG_INF)
            m_new = jnp.maximum(m, jnp.max(s, axis=-1, keepdims=True))
            alpha = jnp.exp2(m - m_new)
            p = jnp.exp2(s - m_new)
            l = alpha * l + jnp.sum(p, axis=-1, keepdims=True)
            a = alpha * a + _dot(p.astype(BF16), v)
            return m_new, l, a

        m1, l1, a1 = one(q1, m1, l1, a1)
        m2, l2, a2 = one(q2, m2, l2, a2)
        return m1, l1, a1, m2, l2, a2

    m0 = jnp.full((tq, 1), NEG_INF, F32)
    l0 = jnp.zeros((tq, 1), F32)
    a0 = jnp.zeros((tq, HEAD_DIM), F32)
    n_full = q_start // tk
    carry = lax.fori_loop(0, n_full, lambda kt, c: tile(kt, c, False), (m0, l0, a0, m0, l0, a0))
    m1, l1, a1, m2, l2, a2 = tile(n_full, carry, True)
    o = a1 / l1 - _lam(lam_ref) * (a2 / l2)
    o_ref[...] = (_head_rms(o, g_ref[...]) * (1.0 - LAM_INIT)).astype(BF16)


def _moba_kernel(slopes_ref, q_ref, qf_ref, k_ref, v_ref, kmean_ref, g_ref, o_ref, *, n_kb, hps, tq):
    hp = pl.program_id(1)
    i = pl.program_id(2)
    blk = MOBA_BLOCK
    hd = HEAD_DIM
    tk = tq
    bpt = tq // blk
    rowi = lax.broadcasted_iota(jnp.int32, (tq, 1), 0)
    coli = lax.broadcasted_iota(jnp.int32, (1, tk), 1)
    bidx_i = lax.broadcasted_iota(jnp.int32, (1, n_kb), 1)
    bidx = bidx_i.astype(F32)
    cq = i * bpt + rowi // blk

    qs, sels, slopes = [], [], []
    for e in range(hps):
        cs = slice(e * hd, (e + 1) * hd)
        gate = lax.dot_general(qf_ref[:, cs], kmean_ref[:, cs], (((1,), (1,)), ((), ())),
                               precision=lax.Precision.HIGHEST, preferred_element_type=F32)
        gate = jnp.where(bidx_i < cq, gate, NEG_INF)
        sel = jnp.zeros((tq, n_kb), F32)
        for _ in range(MOBA_TOPK):
            mx = jnp.max(gate, axis=-1, keepdims=True)
            first = jnp.min(jnp.where(gate == mx, bidx, float(n_kb)), axis=-1, keepdims=True)
            hit = (bidx == first) & (mx > NEG_INF)
            sel = jnp.where(hit, 1.0, sel)
            gate = jnp.where(hit, NEG_INF, gate)
        sels.append(sel)
        qs.append(q_ref[:, cs])
        slopes.append(slopes_ref[hp * hps + e])

    def sel_col(sel, n):
        return jnp.sum(jnp.where(bidx_i == n, sel, 0.0), axis=-1, keepdims=True)

    def tile(t, carry, last):
        k_start = pl.multiple_of(t * tk, tk)
        out = []
        for e in range(hps):
            cs = slice(e * hd, (e + 1) * hd)
            k = k_ref[pl.ds(k_start, tk), cs]
            v = v_ref[pl.ds(k_start, tk), cs]
            s = _dot_nt(qs[e], k) + slopes[e] * (coli + (t - i) * tk).astype(F32)
            if last:
                ok = coli <= rowi
                for hb in range(bpt - 1):
                    need_sel = (rowi // blk > hb) & (coli // blk == hb)
                    picked = sel_col(sels[e], t * bpt + hb) > 0.0
                    ok = ok & (jnp.logical_not(need_sel) | picked)
                s = jnp.where(ok, s, NEG_INF)
            else:
                s = jnp.concatenate(
                    [jnp.where(sel_col(sels[e], t * bpt + hb) > 0.0, s[:, hb * blk:(hb + 1) * blk], NEG_INF)
                     for hb in range(bpt)], axis=1)
            m, l, a = carry[e]
            m_new = jnp.maximum(m, jnp.max(s, axis=-1, keepdims=True))
            m_use = jnp.where(m_new == NEG_INF, 0.0, m_new)
            alpha = jnp.exp2(m - m_use)
            p = jnp.exp2(s - m_use)
            l = alpha * l + jnp.sum(p, axis=-1, keepdims=True)
            a = alpha * a + _dot(p.astype(BF16), v)
            out.append((m_new, l, a))
        return tuple(out)

    init = (jnp.full((tq, 1), NEG_INF, F32), jnp.zeros((tq, 1), F32), jnp.zeros((tq, hd), F32))
    carry = lax.fori_loop(0, i, lambda t, cr: tile(t, cr, False), (init,) * hps)
    carry = tile(i, carry, True)
    outs = []
    for e in range(hps):
        m, l, a = carry[e]
        outs.append(_head_rms(a / l, g_ref[...]))
    o_ref[...] = jnp.concatenate(outs, axis=1).astype(BF16)


def _moba_attn(slopes2, qkv_bf, qm_f32, kmean, out_g, batch, seq, hps=2, tq=512):
    n_kb = seq // MOBA_BLOCK
    nq = seq // tq
    w = hps * HEAD_DIM
    nhp = N_HEADS // hps
    return pl.pallas_call(
        functools.partial(_moba_kernel, n_kb=n_kb, hps=hps, tq=tq),
        grid=(batch, nhp, nq),
        in_specs=[
            pl.BlockSpec(memory_space=pltpu.SMEM),
            pl.BlockSpec((tq, w), lambda b, h, i: (b * nq + i, 3 * nhp + h)),
            pl.BlockSpec((tq, w), lambda b, h, i: (b * nq + i, h)),
            pl.BlockSpec((seq, w), lambda b, h, i: (b, 4 * nhp + h)),
            pl.BlockSpec((seq, w), lambda b, h, i: (b, 5 * nhp + h)),
            pl.BlockSpec((n_kb, w), lambda b, h, i: (b, h)),
            pl.BlockSpec((1, HEAD_DIM), lambda b, h, i: (0, 0)),
        ],
        out_specs=pl.BlockSpec((tq, w), lambda b, h, i: (b * nq + i, h)),
        out_shape=jax.ShapeDtypeStruct((batch * seq, D_GROUP), BF16),
        compiler_params=_params("arbitrary", "arbitrary", "arbitrary"),
        name="moba",
    )(slopes2, qkv_bf, qm_f32, qkv_bf, qkv_bf, kmean, out_g)


def _sweep_body(pp, n_pp, kd_refs, vd_refs, km_refs, q_ref, kn_ref, vn_ref, hmask_ref, tmpl_ref, slope_ref,
                lam_ref, g_ref, o_ref, bsum_ref, m_ref, l_ref, acc_ref, *, pps, past_len):
    nk = pps * PAGE_SIZE
    flat = PAGE_SIZE * N_HEADS

    qm = q_ref[...] * hmask_ref[...]

    @pl.when(pp == 0)
    def _():
        m_ref[...] = jnp.sum(qm * kn_ref[...], axis=-1, keepdims=True)
        l_ref[...] = jnp.ones_like(l_ref)
        acc_ref[...] = vn_ref[...]

    k = jnp.concatenate([r[...].reshape(flat, HEAD_DIM) for r in kd_refs], axis=0).astype(BF16)
    v = jnp.concatenate([r[...].reshape(flat, HEAD_DIM) for r in vd_refs], axis=0).astype(BF16)
    base = jnp.full((1, 1), pp * nk - past_len, jnp.int32).astype(F32)
    s = _dot_nt(qm.astype(BF16), k) + tmpl_ref[...] + slope_ref[...] * base
    m_old = m_ref[...]
    m_new = jnp.maximum(m_old, jnp.max(s, axis=-1, keepdims=True))
    alpha = jnp.exp2(m_old - m_new)
    p = jnp.exp2(s - m_new)
    l_ref[...] = alpha * l_ref[...] + jnp.sum(p, axis=-1, keepdims=True)
    acc_ref[...] = alpha * acc_ref[...] + _dot(p.astype(BF16), v)
    m_ref[...] = m_new

    ppb = MOBA_BLOCK // PAGE_SIZE
    sums = [jnp.sum(r[...], axis=0) for r in km_refs]
    for i in range(pps // ppb):
        bsum_ref[i] = sum(sums[i * ppb:(i + 1) * ppb])

    @pl.when(pp == n_pp - 1)
    def _():
        o16 = acc_ref[...] / l_ref[...]
        o = o16[0:N_HEADS] - _lam(lam_ref) * o16[N_HEADS:2 * N_HEADS]
        o_ref[...] = (_head_rms(o, g_ref[...]) * (1.0 - LAM_INIT)).astype(BF16)


def _diffsweep_kernel(pt_ref, slopes_ref, q_ref, k_ref, v_ref, lam_ref, g_ref, *refs,
                      tq, tk, nq, pps, n_pp, past_len):
    kd_refs = refs[0:pps]
    vd_refs = refs[pps:2 * pps]
    km_refs = refs[2 * pps:3 * pps]
    (qs_ref, kn_ref, vn_ref, hmask_ref, tmpl_ref, slope_ref,
     o_ref, os_ref, bsum_ref, m_ref, l_ref, acc_ref) = refs[3 * pps:]
    b = pl.program_id(0)
    h = pl.program_id(1)
    qi = pl.program_id(2)
    step = (b * N_HEADS + h) * nq + qi
    _sweep_body(step % n_pp, n_pp, kd_refs, vd_refs, km_refs, qs_ref, kn_ref, vn_ref, hmask_ref, tmpl_ref,
                slope_ref, lam_ref, g_ref, os_ref, bsum_ref, m_ref, l_ref, acc_ref, pps=pps, past_len=past_len)
    _diff_body(h, qi, slopes_ref, q_ref, k_ref, v_ref, lam_ref, g_ref, o_ref, tq=tq, tk=tk)


def _diffsweep(page_table, slopes2, qkv_bf, ckd, cvd, ckm, q16, kn16, vn16, lam4, subln_g,
               batch, seq, past_len, tq=256, tk=1024):
    nq = seq // tq
    nb_s, n_pages = page_table.shape
    n_steps = batch * N_HEADS * nq
    pps = nb_s * n_pages // n_steps
    assert pps * n_steps == nb_s * n_pages and n_pages % pps == 0
    n_pp = n_pages // pps
    ppb = MOBA_BLOCK // PAGE_SIZE
    assert pps % ppb == 0
    bps = pps // ppb
    n_rows = 2 * N_HEADS
    n_cols = pps * PAGE_SIZE * N_HEADS

    row = jnp.arange(n_rows)[:, None]
    colj = jnp.arange(n_cols)[None, :]
    slope16 = slopes2[row % N_HEADS]
    lane = jnp.arange(HEAD_DIM)[None, :]
    hmask = ((lane // DQK_DIFF) == (row // N_HEADS)).astype(F32)
    tmpl = jnp.where(colj % N_HEADS == row % N_HEADS,
                     slope16 * (colj // N_HEADS).astype(F32), NEG_INF)

    def step_of(b, h, i):
        return (b * N_HEADS + h) * nq + i

    def page_spec(r):
        def index(b, h, i, pt):
            st = step_of(b, h, i)
            return (0, pt[st // n_pp, (st % n_pp) * pps + r], 0, 0, 0)
        return pl.BlockSpec((None, None, PAGE_SIZE, N_HEADS, HEAD_DIM), index)

    def const_spec(shape):
        return pl.BlockSpec(shape, lambda b, h, i, pt: (0,) * len(shape))

    row_spec = pl.BlockSpec((None, n_rows, HEAD_DIM), lambda b, h, i, pt: (step_of(b, h, i) // n_pp, 0, 0))
    return pl.pallas_call(
        functools.partial(_diffsweep_kernel, tq=tq, tk=tk, nq=nq, pps=pps, n_pp=n_pp, past_len=past_len),
        grid_spec=pltpu.PrefetchScalarGridSpec(
            num_scalar_prefetch=1,
            grid=(batch, N_HEADS, nq),
            in_specs=[
                pl.BlockSpec(memory_space=pltpu.SMEM),
                pl.BlockSpec((tq, HEAD_DIM), lambda b, h, i, pt: (b * nq + i, h)),
                pl.BlockSpec((seq, HEAD_DIM), lambda b, h, i, pt: (b, N_HEADS + h)),
                pl.BlockSpec((seq, HEAD_DIM), lambda b, h, i, pt: (b, 2 * N_HEADS + h)),
                const_spec((4, DQK_DIFF)), const_spec((1, HEAD_DIM))]
            + [page_spec(r) for r in range(pps)] * 3
            + [row_spec, row_spec, row_spec,
               const_spec((n_rows, HEAD_DIM)), const_spec((n_rows, n_cols)), const_spec((n_rows, 1))],
            out_specs=[
                pl.BlockSpec((tq, HEAD_DIM), lambda b, h, i, pt: (b * nq + i, h)),
                pl.BlockSpec((None, N_HEADS, HEAD_DIM), lambda b, h, i, pt: (step_of(b, h, i) // n_pp, 0, 0)),
                pl.BlockSpec((None, bps, N_HEADS, HEAD_DIM),
                             lambda b, h, i, pt: (step_of(b, h, i) // n_pp, step_of(b, h, i) % n_pp, 0, 0))],
            scratch_shapes=[pltpu.VMEM((n_rows, 1), F32), pltpu.VMEM((n_rows, 1), F32),
                            pltpu.VMEM((n_rows, HEAD_DIM), F32)],
        ),
        out_shape=[jax.ShapeDtypeStruct((batch * seq, D_GROUP), BF16),
                   jax.ShapeDtypeStruct((nb_s, N_HEADS, HEAD_DIM), BF16),
                   jax.ShapeDtypeStruct((nb_s, n_pages // ppb, N_HEADS, HEAD_DIM), F32)],
        compiler_params=_params("arbitrary", "arbitrary", "arbitrary"),
        name="diffsweep",
    )(page_table, slopes2, qkv_bf, qkv_bf, qkv_bf, lam4, subln_g,
      *([ckd] * pps), *([cvd] * pps), *([ckm] * pps), q16, kn16, vn16, hmask, tmpl, slope16)


def _gate_kernel(q_ref, bsum_ref, sel_ref, *, n_blk):
    kmean = bsum_ref[...] * (1.0 / MOBA_BLOCK)
    gate = jnp.sum(kmean * q_ref[...], axis=-1, keepdims=True)
    bidx = lax.broadcasted_iota(jnp.int32, gate.shape, 0).astype(F32)
    for r in range(MOBA_TOPK):
        mx = jnp.max(gate, axis=0, keepdims=True)
        first = jnp.min(jnp.where(gate == mx, bidx, float(n_blk)), axis=0, keepdims=True)
        sel_ref[r] = first[0].astype(jnp.int32)
        gate = jnp.where(bidx == first, NEG_INF, gate)


def _gate(qm8, bsum):
    nb_s, n_blk = bsum.shape[:2]
    assert n_blk >= MOBA_TOPK
    return pl.pallas_call(
        functools.partial(_gate_kernel, n_blk=n_blk),
        grid=(nb_s,),
        in_specs=[pl.BlockSpec((None, N_HEADS, HEAD_DIM), lambda b: (b, 0, 0)),
                  pl.BlockSpec((None, n_blk, N_HEADS, HEAD_DIM), lambda b: (b, 0, 0, 0))],
        out_specs=pl.BlockSpec((None, MOBA_TOPK, N_HEADS, 1), lambda b: (b, 0, 0, 0)),
        out_shape=jax.ShapeDtypeStruct((nb_s, MOBA_TOPK, N_HEADS, 1), jnp.int32),
        compiler_params=_params("arbitrary"),
        name="gate",
    )(qm8, bsum)


def _msel_kernel(sel_ref, pt_ref, slopes_ref, ck_ref, cv_ref, q_ref, kn_ref, vn_ref, g_ref, o_ref,
                 kbuf, vbuf, sem, *, past_len, nb_s):
    b = pl.program_id(0)
    ppb = MOBA_BLOCK // PAGE_SIZE

    def copies(bb, slot):
        out = []
        for h in range(N_HEADS):
            for kk in range(MOBA_TOPK):
                blk = sel_ref[bb, kk * N_HEADS + h]
                for jj in range(ppb):
                    page = pt_ref[bb, blk * ppb + jj]
                    rows = pl.ds((kk * ppb + jj) * PAGE_SIZE, PAGE_SIZE)
                    out.append(pltpu.make_async_copy(ck_ref.at[0, page, :, h, :], kbuf.at[slot, h, rows, :],
                                                     sem.at[slot]))
                    out.append(pltpu.make_async_copy(cv_ref.at[0, page, :, h, :], vbuf.at[slot, h, rows, :],
                                                     sem.at[slot]))
        return out

    @pl.when(b == 0)
    def _():
        for c in copies(0, 0):
            c.start()

    @pl.when(b + 1 < nb_s)
    def _():
        for c in copies(b + 1, (b + 1) % 2):
            c.start()

    slot = b % 2
    for c in copies(b, slot):
        c.wait()

    r_iota = lax.broadcasted_iota(jnp.int32, (1, MOBA_BLOCK), 1)
    outs = []
    for h in range(N_HEADS):
        q = q_ref[h:h + 1, :]
        k = kbuf[slot, h].astype(BF16)
        v = vbuf[slot, h].astype(BF16)
        q8 = jnp.broadcast_to(q, (8, HEAD_DIM)).astype(BF16)
        dist = jnp.concatenate(
            [past_len - (sel_ref[b, kk * N_HEADS + h] * MOBA_BLOCK + r_iota) for kk in range(MOBA_TOPK)], axis=1)
        s = _dot_nt(q8, k)[0:1] - slopes_ref[h] * dist.astype(F32)
        s_self = jnp.sum(q * kn_ref[h:h + 1, :], axis=-1, keepdims=True)
        m = jnp.maximum(jnp.max(s, axis=-1, keepdims=True), s_self)
        p = jnp.exp2(s - m)
        p_self = jnp.exp2(s_self - m)
        denom = jnp.sum(p, axis=-1, keepdims=True) + p_self
        p8 = jnp.broadcast_to(p, (8, p.shape[1])).astype(BF16)
        o = (_dot(p8, v)[0:1] + p_self * vn_ref[h:h + 1, :]) / denom
        outs.append(_head_rms(o, g_ref[...]))
    o_ref[...] = jnp.concatenate(outs, axis=0).astype(BF16)


def _msel(sel, page_table, slopes2, ckm, cvm, q_s, kn_s, vn_s, out_g, past_len):
    nb_s = page_table.shape[0]
    n_keys = MOBA_TOPK * MOBA_BLOCK
    head_spec = pl.BlockSpec((None, N_HEADS, HEAD_DIM), lambda b, sel, pt: (b, 0, 0))
    return pl.pallas_call(
        functools.partial(_msel_kernel, past_len=past_len, nb_s=nb_s),
        grid_spec=pltpu.PrefetchScalarGridSpec(
            num_scalar_prefetch=2,
            grid=(nb_s,),
            in_specs=[pl.BlockSpec(memory_space=pltpu.SMEM),
                      pl.BlockSpec(memory_space=pl.ANY), pl.BlockSpec(memory_space=pl.ANY),
                      head_spec, head_spec, head_spec,
                      pl.BlockSpec((1, HEAD_DIM), lambda b, sel, pt: (0, 0))],
            out_specs=head_spec,
            scratch_shapes=[pltpu.VMEM((2, N_HEADS, n_keys, HEAD_DIM), F32),
                            pltpu.VMEM((2, N_HEADS, n_keys, HEAD_DIM), F32),
                            pltpu.SemaphoreType.DMA((2,))],
        ),
        out_shape=jax.ShapeDtypeStruct((nb_s, N_HEADS, HEAD_DIM), BF16),
        compiler_params=_params("arbitrary"),
        name="msel",
    )(sel, page_table, slopes2, ckm, cvm, q_s, kn_s, vn_s, out_g)


def _oproj_kernel(od_ref, om_ref, x_ref, wa_ref, wb_ref, g_ref, h_ref, hn_ref):
    hval = x_ref[...] + _dot(od_ref[...], wa_ref[...]) + _dot(om_ref[...], wb_ref[...])
    h_ref[...] = hval
    ms = jnp.mean(hval * hval, axis=-1, keepdims=True)
    hn_ref[...] = (hval * lax.rsqrt(ms + EPS) * g_ref[...]).astype(BF16)


def _oproj(od, om, x, w_out_bf, mlp_g, tm):
    rows = x.shape[0]
    return pl.pallas_call(
        _oproj_kernel,
        grid=(rows // tm,),
        in_specs=[
            pl.BlockSpec((tm, D_GROUP), lambda i: (i, 0)),
            pl.BlockSpec((tm, D_GROUP), lambda i: (i, 0)),
            pl.BlockSpec((tm, D_MODEL), lambda i: (i, 0)),
            pl.BlockSpec((D_GROUP, D_MODEL), lambda i: (0, 0)),
            pl.BlockSpec((D_GROUP, D_MODEL), lambda i: (1, 0)),
            pl.BlockSpec((1, D_MODEL), lambda i: (0, 0)),
        ],
        out_specs=[pl.BlockSpec((tm, D_MODEL), lambda i: (i, 0)),
                   pl.BlockSpec((tm, D_MODEL), lambda i: (i, 0))],
        out_shape=[jax.ShapeDtypeStruct((rows, D_MODEL), F32),
                   jax.ShapeDtypeStruct((rows, D_MODEL), BF16)],
        compiler_params=_params("arbitrary"),
        name="oproj",
    )(od, om, x, w_out_bf, w_out_bf, mlp_g)


def _mlp_kernel(hn_ref, h_ref, wu_ref, wd_ref, y_ref):
    f = pl.program_id(1)
    u = jnp.maximum(_dot(hn_ref[...], wu_ref[...]), 0.0)
    contrib = _dot((u * u).astype(BF16), wd_ref[...])

    @pl.when(f == 0)
    def _():
        y_ref[...] = h_ref[...] + contrib

    @pl.when(f > 0)
    def _():
        y_ref[...] += contrib


def _mlp(hn, h, w_up_bf, w_down_bf, tm, tf):
    rows = h.shape[0]
    return pl.pallas_call(
        _mlp_kernel,
        grid=(rows // tm, D_FF // tf),
        in_specs=[
            pl.BlockSpec((tm, D_MODEL), lambda i, f: (i, 0)),
            pl.BlockSpec((tm, D_MODEL), lambda i, f: (i, 0)),
            pl.BlockSpec((D_MODEL, tf), lambda i, f: (0, f)),
            pl.BlockSpec((tf, D_MODEL), lambda i, f: (f, 0)),
        ],
        out_specs=pl.BlockSpec((tm, D_MODEL), lambda i, f: (i, 0)),
        out_shape=jax.ShapeDtypeStruct((rows, D_MODEL), F32),
        compiler_params=_params("arbitrary", "arbitrary"),
        name="mlp",
    )(hn, h, w_up_bf, w_down_bf)


def _alibi_slopes(n):
    return jnp.exp2(-8.0 * jnp.arange(1, n + 1, dtype=F32) / n)


def kernel(x_prompt, x_sample, cache_k_diff, cache_v_diff, cache_k_moba, cache_v_moba, page_table,
           attn_norm_g, w_in, qn_diff_g, kn_diff_g, qn_moba_g, kn_moba_g,
           lambda_q1, lambda_k1, lambda_q2, lambda_k2, subln_diff_g, out_norm_moba_g, w_out,
           mlp_norm_g, w_up, w_down):
    bp, sp, _ = x_prompt.shape
    bs, ss, _ = x_sample.shape
    assert ss == 1 and w_in.shape[0] == 1
    past_len = page_table.shape[1] * PAGE_SIZE
    assert past_len % MOBA_BLOCK == 0
    layer = 0

    slopes = _alibi_slopes(N_HEADS)
    slopes2 = slopes * LOG2E
    w_in_bf = w_in[layer].astype(BF16)
    w_out_bf = w_out[layer].astype(BF16)
    w_up_bf = w_up[layer].astype(BF16)
    w_down_bf = w_down[layer].astype(BF16)
    attn_g = attn_norm_g[layer].reshape(1, D_MODEL)
    mlp_g = mlp_norm_g[layer].reshape(1, D_MODEL)
    subln_g = subln_diff_g[layer].reshape(1, HEAD_DIM)
    out_g = out_norm_moba_g[layer].reshape(1, HEAD_DIM)
    lam4 = jnp.stack([lambda_q1[layer], lambda_k1[layer], lambda_q2[layer], lambda_k2[layer]]).astype(F32)

    ones = jnp.ones((HEAD_DIM,), F32)
    qd_g = jnp.tile(qn_diff_g[layer], 2)
    kd_g = jnp.tile(kn_diff_g[layer], 2)
    gains = jnp.stack([qd_g, kd_g, ones, qn_moba_g[layer], kn_moba_g[layer], ones]).reshape(6, 1, HEAD_DIM)
    scales = jnp.array([DQK_DIFF ** -0.5 * LOG2E, 1.0, 1.0, HEAD_DIM ** -0.5 * LOG2E, 1.0, 1.0], F32).reshape(6, 1, 1)
    gains_b = gains * scales
    lane = jnp.arange(HEAD_DIM)
    bm_diff = (lane[:, None] // DQK_DIFF == lane[None, :] // DQK_DIFF).astype(F32) / DQK_DIFF
    bm_moba = jnp.full((HEAD_DIM, HEAD_DIM), 1.0 / HEAD_DIM, F32)
    bmats = jnp.stack([bm_diff, bm_moba]).astype(BF16)

    def dense_tail(od, om, x2d, tm, tf):
        h, hn = _oproj(od, om, x2d, w_out_bf, mlp_g, tm)
        return _mlp(hn, h, w_up_bf, w_down_bf, tm, tf)

    xp2 = x_prompt.reshape(bp * sp, D_MODEL)
    kd_p, vd_p, qm_p, km_p, vm_p, qkv_p = _proj(xp2, attn_g, w_in_bf, gains, gains_b, bmats, tm=512)
    xs2 = x_sample.reshape(bs, D_MODEL)
    kd_s, vd_s, qm_s, km_s, vm_s, qkv_s = _proj(xs2, attn_g, w_in_bf, gains, gains_b, bmats, tm=bs)
    heads = lambda a: a.reshape(bs, N_HEADS, HEAD_DIM)
    twice = lambda a: jnp.concatenate([heads(a), heads(a)], axis=1)
    qd_s = qkv_s[:, :D_GROUP].astype(F32)
    qms_scaled = qkv_s[:, 3 * D_GROUP:4 * D_GROUP].astype(F32)

    od_p, od_s, bsum = _diffsweep(page_table, slopes2, qkv_p, cache_k_diff, cache_v_diff, cache_k_moba,
                                  twice(qd_s), twice(kd_s), twice(vd_s), lam4, subln_g, bp, sp, past_len)

    kmean_p = _kmean(km_p)
    om_p = _moba_attn(slopes2, qkv_p, qm_p, kmean_p, out_g, bp, sp)
    y_prompt = dense_tail(od_p, om_p, xp2, 512, 1024).reshape(bp, sp, D_MODEL)

    sel = _gate(heads(qm_s), bsum).reshape(bs, MOBA_TOPK * N_HEADS)
    om_s = _msel(sel, page_table, slopes2, cache_k_moba, cache_v_moba,
                 heads(qms_scaled), heads(km_s), heads(vm_s), out_g, past_len)
    y_sample = dense_tail(od_s.reshape(bs, D_GROUP), om_s.reshape(bs, D_GROUP), xs2, bs, 1024)
    y_sample = y_sample.reshape(bs, ss, D_MODEL)

    kv_p = lambda a: a.reshape(1, bp, sp, N_HEADS, HEAD_DIM)
    kv_s = lambda a: a.reshape(1, bs, ss, N_HEADS, HEAD_DIM)
    return (y_prompt, y_sample, kv_p(kd_p), kv_p(vd_p), kv_p(km_p), kv_p(vm_p),
            kv_s(kd_s), kv_s(vd_s), kv_s(km_s), kv_s(vm_s))
```

```python
import functools
import math

import jax
import jax.numpy as jnp
from jax import lax
from jax.experimental import pallas as pl
from jax.experimental.pallas import tpu as pltpu

F32 = jnp.float32
BF16 = jnp.bfloat16

D_MODEL = 2048
HEAD_DIM = 128
N_HEADS = 8
D_GROUP = N_HEADS * HEAD_DIM
DQK_DIFF = HEAD_DIM // 2
MOBA_BLOCK = 256
MOBA_TOPK = 3
PAGE_SIZE = 128
D_FF = 4 * D_MODEL
EPS = 1e-6
NEG_INF = float("-inf")
LAM_INIT = 0.8 - 0.6 * math.exp(-0.3 * 0)
LOG2E = math.log2(math.e)

VMEM_LIMIT = 56 * 1024 * 1024


def _dot(a, b):
    return jnp.dot(a, b, preferred_element_type=F32)


def _dot_nt(a, b):
    return lax.dot_general(a, b, (((1,), (1,)), ((), ())), preferred_element_type=F32)


def _params(*sem):
    return pltpu.CompilerParams(dimension_semantics=sem, vmem_limit_bytes=VMEM_LIMIT)


def _proj_kernel(x_ref, g_ref, w_ref, gain_ref, gainb_ref, bmat_ref,
                 kd_ref, vd_ref, qm_ref, km_ref, vm_ref, bf_ref, xn_ref):
    j = pl.program_id(1)

    @pl.when(j == 0)
    def _():
        x = x_ref[...]
        ms = jnp.mean(x * x, axis=-1, keepdims=True)
        xn_ref[...] = (x * lax.rsqrt(ms + EPS) * g_ref[...]).astype(BF16)

    z = _dot(xn_ref[...], w_ref[...])
    is_v = (j % 3) == 2
    bmat = bmat_ref[0]
    ys, ybs = [], []
    for hh in range(N_HEADS):
        zc = z[:, hh * HEAD_DIM:(hh + 1) * HEAD_DIM]
        ms = _dot((zc * zc).astype(BF16), bmat)
        rs = jnp.where(is_v, 1.0, lax.rsqrt(ms + EPS))
        zr = zc * rs
        ys.append(zr * gain_ref[0])
        ybs.append((zr * gainb_ref[0]).astype(BF16))
    y = jnp.concatenate(ys, axis=1)
    bf_ref[...] = jnp.concatenate(ybs, axis=1)

    for jj, ref in ((1, kd_ref), (2, vd_ref), (3, qm_ref), (4, km_ref), (5, vm_ref)):
        @pl.when(j == jj)
        def _(ref=ref):
            ref[...] = y


def _proj(x, g, w_bf, gains, gains_b, bmats, tm):
    rows = x.shape[0]
    n_i = rows // tm
    f32_out = jax.ShapeDtypeStruct((rows, D_GROUP), F32)
    f32_spec = pl.BlockSpec((tm, D_GROUP), lambda i, j: (i, 0))
    return pl.pallas_call(
        _proj_kernel,
        grid=(n_i, 6),
        in_specs=[
            pl.BlockSpec((tm, D_MODEL), lambda i, j: (i, 0)),
            pl.BlockSpec((1, D_MODEL), lambda i, j: (0, 0)),
            pl.BlockSpec((D_MODEL, D_GROUP), lambda i, j: (0, j)),
            pl.BlockSpec((1, 1, HEAD_DIM), lambda i, j: (j, 0, 0)),
            pl.BlockSpec((1, 1, HEAD_DIM), lambda i, j: (j, 0, 0)),
            pl.BlockSpec((1, HEAD_DIM, HEAD_DIM), lambda i, j: (j // 3, 0, 0)),
        ],
        out_specs=[f32_spec] * 5 + [pl.BlockSpec((tm, D_GROUP), lambda i, j: (i, j))],
        out_shape=[f32_out] * 5 + [jax.ShapeDtypeStruct((rows, 6 * D_GROUP), BF16)],
        scratch_shapes=[pltpu.VMEM((tm, D_MODEL), BF16)],
        compiler_params=_params("arbitrary", "arbitrary"),
        name="proj",
    )(x, g, w_bf, gains, gains_b, bmats)


def _qgate_kernel(x_ref, g_ref, w_ref, gain_ref, o_ref):
    x = x_ref[...]
    xn = x * lax.rsqrt(jnp.mean(x * x, axis=-1, keepdims=True) + EPS) * g_ref[...]
    z = jnp.dot(xn, w_ref[...], precision=lax.Precision.HIGHEST, preferred_element_type=F32)
    o_ref[...] = jnp.concatenate(
        [_head_rms(z[:, hh * HEAD_DIM:(hh + 1) * HEAD_DIM], gain_ref[...]) for hh in range(N_HEADS)], axis=1)


def _qgate(x, g, w_in_f32, qn_g):
    rows = x.shape[0]
    return pl.pallas_call(
        _qgate_kernel,
        grid=(1,),
        in_specs=[
            pl.BlockSpec((rows, D_MODEL), lambda i: (0, 0)),
            pl.BlockSpec((1, D_MODEL), lambda i: (0, 0)),
            pl.BlockSpec((D_MODEL, D_GROUP), lambda i: (0, 3)),
            pl.BlockSpec((1, HEAD_DIM), lambda i: (0, 0)),
        ],
        out_specs=pl.BlockSpec((rows, D_GROUP), lambda i: (0, 0)),
        out_shape=jax.ShapeDtypeStruct((rows, D_GROUP), F32),
        compiler_params=_params("arbitrary"),
        name="qgate",
    )(x, g, w_in_f32, qn_g)


def _kmean_kernel(k_ref, o_ref, *, nb):
    k = k_ref[...].reshape(nb, MOBA_BLOCK, D_GROUP)
    o_ref[...] = jnp.sum(k, axis=1) * (1.0 / MOBA_BLOCK)


def _kmean(km, nb=8):
    rows = km.shape[0]
    return pl.pallas_call(
        functools.partial(_kmean_kernel, nb=nb),
        grid=(rows // (nb * MOBA_BLOCK),),
        in_specs=[pl.BlockSpec((nb * MOBA_BLOCK, D_GROUP), lambda i: (i, 0))],
        out_specs=pl.BlockSpec((nb, D_GROUP), lambda i: (i, 0)),
        out_shape=jax.ShapeDtypeStruct((rows // MOBA_BLOCK, D_GROUP), F32),
        compiler_params=_params("arbitrary"),
        name="kmean",
    )(km)


def _lam(lam_ref):
    l = lam_ref[...]
    a = jnp.sum(l[0:1] * l[1:2], axis=-1, keepdims=True)
    b = jnp.sum(l[2:3] * l[3:4], axis=-1, keepdims=True)
    return jnp.exp(a) - jnp.exp(b) + LAM_INIT


def _head_rms(o, gain):
    return o * lax.rsqrt(jnp.mean(o * o, axis=-1, keepdims=True) + EPS) * gain


def _diff_body(h, qi, slopes_ref, q_ref, k_ref, v_ref, lam_ref, g_ref, o_ref, *, tq, tk, side_work=None):
    slope = slopes_ref[h]
    q = q_ref[...]
    lane = lax.broadcasted_iota(jnp.int32, (1, HEAD_DIM), 1)
    zero = jnp.zeros_like(q)
    q1 = jnp.where(lane < DQK_DIFF, q, zero)
    q2 = jnp.where(lane >= DQK_DIFF, q, zero)
    q_start = qi * tq
    col = lax.broadcasted_iota(jnp.int32, (1, tk), 1)

    def tile(kt, carry, masked):
        m1, l1, a1, m2, l2, a2 = carry
        k_start = pl.multiple_of(kt * tk, tk)
        k = k_ref[pl.ds(k_start, tk), :]
        v = v_ref[pl.ds(k_start, tk), :]
        bias = slope * (col + (k_start - q_start)).astype(F32)
        if masked:
            row = lax.broadcasted_iota(jnp.int32, (tq, tk), 0)
            ok = (row + q_start) >= (col + k_start)

        def one(qc, m, l, a):
            s = _dot_nt(qc, k) + bias
            if masked:
                s = jnp.where(ok, s, NEG_INF)
            m_new = jnp.maximum(m, jnp.max(s, axis=-1, keepdims=True))
            alpha = jnp.exp2(m - m_new)
            p = jnp.exp2(s - m_new)
            l = alpha * l + jnp.sum(p, axis=-1, keepdims=True)
            a = alpha * a + _dot(p.astype(BF16), v)
            return m_new, l, a

        m1, l1, a1 = one(q1, m1, l1, a1)
        m2, l2, a2 = one(q2, m2, l2, a2)
        return m1, l1, a1, m2, l2, a2

    m0 = jnp.full((tq, 1), NEG_INF, F32)
    l0 = jnp.zeros((tq, 1), F32)
    a0 = jnp.zeros((tq, HEAD_DIM), F32)
    n_full = q_start // tk
    carry = lax.fori_loop(0, n_full, lambda kt, c: tile(kt, c, False), (m0, l0, a0, m0, l0, a0))
    if side_work is not None:
        side_work()
    m1, l1, a1, m2, l2, a2 = tile(n_full, carry, True)
    o = a1 / l1 - _lam(lam_ref) * (a2 / l2)
    o_ref[...] = (_head_rms(o, g_ref[...]) * (1.0 - LAM_INIT)).astype(BF16)


def _moba_kernel(slopes_ref, q_ref, qf_ref, k_ref, v_ref, kmean_ref, g_ref, o_ref, *, n_kb, hps, tq):
    hp = pl.program_id(1)
    i = pl.program_id(2)
    blk = MOBA_BLOCK
    hd = HEAD_DIM
    tk = tq
    bpt = tq // blk
    rowi = lax.broadcasted_iota(jnp.int32, (tq, 1), 0)
    coli = lax.broadcasted_iota(jnp.int32, (1, tk), 1)
    bidx_i = lax.broadcasted_iota(jnp.int32, (1, n_kb), 1)
    bidx = bidx_i.astype(F32)
    cq = i * bpt + rowi // blk

    qs, sels, slopes = [], [], []
    for e in range(hps):
        cs = slice(e * hd, (e + 1) * hd)
        gate = lax.dot_general(qf_ref[:, cs], kmean_ref[:, cs], (((1,), (1,)), ((), ())),
                               precision=lax.Precision.HIGHEST, preferred_element_type=F32)
        gate = jnp.where(bidx_i < cq, gate, NEG_INF)
        sel = jnp.zeros((tq, n_kb), F32)
        for _ in range(MOBA_TOPK):
            mx = jnp.max(gate, axis=-1, keepdims=True)
            first = jnp.min(jnp.where(gate == mx, bidx, float(n_kb)), axis=-1, keepdims=True)
            hit = (bidx == first) & (mx > NEG_INF)
            sel = jnp.where(hit, 1.0, sel)
            gate = jnp.where(hit, NEG_INF, gate)
        sels.append(sel)
        qs.append(q_ref[:, cs])
        slopes.append(slopes_ref[hp * hps + e])

    def sel_col(sel, n):
        return jnp.sum(jnp.where(bidx_i == n, sel, 0.0), axis=-1, keepdims=True)

    def tile(t, carry, last):
        k_start = pl.multiple_of(t * tk, tk)
        out = []
        for e in range(hps):
            cs = slice(e * hd, (e + 1) * hd)
            k = k_ref[pl.ds(k_start, tk), cs]
            v = v_ref[pl.ds(k_start, tk), cs]
            s = _dot_nt(qs[e], k) + slopes[e] * (coli + (t - i) * tk).astype(F32)
            if last:
                ok = coli <= rowi
                for hb in range(bpt - 1):
                    need_sel = (rowi // blk > hb) & (coli // blk == hb)
                    picked = sel_col(sels[e], t * bpt + hb) > 0.0
                    ok = ok & (jnp.logical_not(need_sel) | picked)
                s = jnp.where(ok, s, NEG_INF)
            else:
                s = jnp.concatenate(
                    [jnp.where(sel_col(sels[e], t * bpt + hb) > 0.0, s[:, hb * blk:(hb + 1) * blk], NEG_INF)
                     for hb in range(bpt)], axis=1)
            m, l, a = carry[e]
            m_new = jnp.maximum(m, jnp.max(s, axis=-1, keepdims=True))
            m_use = jnp.where(m_new == NEG_INF, 0.0, m_new)
            alpha = jnp.exp2(m - m_use)
            p = jnp.exp2(s - m_use)
            l = alpha * l + jnp.sum(p, axis=-1, keepdims=True)
            a = alpha * a + _dot(p.astype(BF16), v)
            out.append((m_new, l, a))
        return tuple(out)

    init = (jnp.full((tq, 1), NEG_INF, F32), jnp.zeros((tq, 1), F32), jnp.zeros((tq, hd), F32))
    carry = lax.fori_loop(0, i, lambda t, cr: tile(t, cr, False), (init,) * hps)
    carry = tile(i, carry, True)
    outs = []
    for e in range(hps):
        m, l, a = carry[e]
        outs.append(_head_rms(a / l, g_ref[...]))
    o_ref[...] = jnp.concatenate(outs, axis=1).astype(BF16)


def _moba_attn(slopes2, qkv_bf, qm_f32, kmean, out_g, batch, seq, hps=2, tq=512):
    n_kb = seq // MOBA_BLOCK
    nq = seq // tq
    w = hps * HEAD_DIM
    nhp = N_HEADS // hps
    return pl.pallas_call(
        functools.partial(_moba_kernel, n_kb=n_kb, hps=hps, tq=tq),
        grid=(batch, nhp, nq),
        in_specs=[
            pl.BlockSpec(memory_space=pltpu.SMEM),
            pl.BlockSpec((tq, w), lambda b, h, i: (b * nq + i, 3 * nhp + h)),
            pl.BlockSpec((tq, w), lambda b, h, i: (b * nq + i, h)),
            pl.BlockSpec((seq, w), lambda b, h, i: (b, 4 * nhp + h)),
            pl.BlockSpec((seq, w), lambda b, h, i: (b, 5 * nhp + h)),
            pl.BlockSpec((n_kb, w), lambda b, h, i: (b, h)),
            pl.BlockSpec((1, HEAD_DIM), lambda b, h, i: (0, 0)),
        ],
        out_specs=pl.BlockSpec((tq, w), lambda b, h, i: (b * nq + i, h)),
        out_shape=jax.ShapeDtypeStruct((batch * seq, D_GROUP), BF16),
        compiler_params=_params("arbitrary", "arbitrary", "arbitrary"),
        name="moba",
    )(slopes2, qkv_bf, qm_f32, qkv_bf, qkv_bf, kmean, out_g)


def _sweep_init(pp, q_ref, kn_ref, vn_ref, hmask_ref, m_ref, l_ref, acc_ref):
    @pl.when(pp == 0)
    def _():
        qm = q_ref[...] * hmask_ref[...]
        m_ref[...] = jnp.sum(qm * kn_ref[...], axis=-1, keepdims=True)
        l_ref[...] = jnp.ones_like(l_ref)
        acc_ref[...] = vn_ref[...]


def _sweep_step(pp, kd_refs, vd_refs, km_refs, q_ref, hmask_ref, tmpl_ref, slope_ref,
                lam_ref, g_ref, o_ref, bsum_ref, m_ref, l_ref, acc_ref, *, pps, past_len):
    nk = pps * PAGE_SIZE
    flat = PAGE_SIZE * N_HEADS

    qm = q_ref[...] * hmask_ref[...]
    k = jnp.concatenate([r[...].reshape(flat, HEAD_DIM) for r in kd_refs], axis=0).astype(BF16)
    v = jnp.concatenate([r[...].reshape(flat, HEAD_DIM) for r in vd_refs], axis=0).astype(BF16)
    base = jnp.full((1, 1), pp * nk - past_len, jnp.int32).astype(F32)
    s = _dot_nt(qm.astype(BF16), k) + tmpl_ref[...] + slope_ref[...] * base
    m_old = m_ref[...]
    m_new = jnp.maximum(m_old, jnp.max(s, axis=-1, keepdims=True))
    alpha = jnp.exp2(m_old - m_new)
    p = jnp.exp2(s - m_new)
    l_new = alpha * l_ref[...] + jnp.sum(p, axis=-1, keepdims=True)
    acc_new = alpha * acc_ref[...] + _dot(p.astype(BF16), v)
    l_ref[...] = l_new
    acc_ref[...] = acc_new
    m_ref[...] = m_new

    ppb = MOBA_BLOCK // PAGE_SIZE
    sums = [jnp.sum(r[...], axis=0) for r in km_refs]
    for i in range(pps // ppb):
        bsum_ref[i] = sum(sums[i * ppb:(i + 1) * ppb])

    o16 = acc_new / l_new
    o = o16[0:N_HEADS] - _lam(lam_ref) * o16[N_HEADS:2 * N_HEADS]
    o_ref[...] = (_head_rms(o, g_ref[...]) * (1.0 - LAM_INIT)).astype(BF16)


def _diffsweep_kernel(pt_ref, slopes_ref, q_ref, k_ref, v_ref, lam_ref, g_ref, *refs,
                      tq, tk, nq, pps, n_pp, past_len):
    kd_refs = refs[0:pps]
    vd_refs = refs[pps:2 * pps]
    km_refs = refs[2 * pps:3 * pps]
    (qs_ref, kn_ref, vn_ref, hmask_ref, tmpl_ref, slope_ref,
     o_ref, os_ref, bsum_ref, m_ref, l_ref, acc_ref) = refs[3 * pps:]
    b = pl.program_id(0)
    h = pl.program_id(1)
    qi = pl.program_id(2)
    step = (b * N_HEADS + h) * nq + qi
    pp = step % n_pp
    _sweep_init(pp, qs_ref, kn_ref, vn_ref, hmask_ref, m_ref, l_ref, acc_ref)
    sweep = functools.partial(
        _sweep_step, pp, kd_refs, vd_refs, km_refs, qs_ref, hmask_ref, tmpl_ref, slope_ref,
        lam_ref, g_ref, os_ref, bsum_ref, m_ref, l_ref, acc_ref, pps=pps, past_len=past_len)
    _diff_body(h, qi, slopes_ref, q_ref, k_ref, v_ref, lam_ref, g_ref, o_ref, tq=tq, tk=tk, side_work=sweep)


def _diffsweep(page_table, slopes2, qkv_bf, ckd, cvd, ckm, q16, kn16, vn16, lam4, subln_g,
               batch, seq, past_len, tq=256, tk=1024):
    nq = seq // tq
    nb_s, n_pages = page_table.shape
    n_steps = batch * N_HEADS * nq
    pps = nb_s * n_pages // n_steps
    assert pps * n_steps == nb_s * n_pages and n_pages % pps == 0
    n_pp = n_pages // pps
    ppb = MOBA_BLOCK // PAGE_SIZE
    assert pps % ppb == 0
    bps = pps // ppb
    n_rows = 2 * N_HEADS
    n_cols = pps * PAGE_SIZE * N_HEADS

    row = jnp.arange(n_rows)[:, None]
    colj = jnp.arange(n_cols)[None, :]
    slope16 = slopes2[row % N_HEADS]
    lane = jnp.arange(HEAD_DIM)[None, :]
    hmask = ((lane // DQK_DIFF) == (row // N_HEADS)).astype(F32)
    tmpl = jnp.where(colj % N_HEADS == row % N_HEADS,
                     slope16 * (colj // N_HEADS).astype(F32), NEG_INF)

    def step_of(b, h, i):
        return (b * N_HEADS + h) * nq + i

    def page_spec(r):
        def index(b, h, i, pt):
            st = step_of(b, h, i)
            return (0, pt[st // n_pp, (st % n_pp) * pps + r], 0, 0, 0)
        return pl.BlockSpec((None, None, PAGE_SIZE, N_HEADS, HEAD_DIM), index)

    def const_spec(shape):
        return pl.BlockSpec(shape, lambda b, h, i, pt: (0,) * len(shape))

    row_spec = pl.BlockSpec((None, n_rows, HEAD_DIM), lambda b, h, i, pt: (step_of(b, h, i) // n_pp, 0, 0))
    return pl.pallas_call(
        functools.partial(_diffsweep_kernel, tq=tq, tk=tk, nq=nq, pps=pps, n_pp=n_pp, past_len=past_len),
        grid_spec=pltpu.PrefetchScalarGridSpec(
            num_scalar_prefetch=1,
            grid=(batch, N_HEADS, nq),
            in_specs=[
                pl.BlockSpec(memory_space=pltpu.SMEM),
                pl.BlockSpec((tq, HEAD_DIM), lambda b, h, i, pt: (b * nq + i, h)),
                pl.BlockSpec((seq, HEAD_DIM), lambda b, h, i, pt: (b, N_HEADS + h)),
                pl.BlockSpec((seq, HEAD_DIM), lambda b, h, i, pt: (b, 2 * N_HEADS + h)),
                const_spec((4, DQK_DIFF)), const_spec((1, HEAD_DIM))]
            + [page_spec(r) for r in range(pps)] * 3
            + [row_spec, row_spec, row_spec,
               const_spec((n_rows, HEAD_DIM)), const_spec((n_rows, n_cols)), const_spec((n_rows, 1))],
            out_specs=[
                pl.BlockSpec((tq, HEAD_DIM), lambda b, h, i, pt: (b * nq + i, h)),
                pl.BlockSpec((None, N_HEADS, HEAD_DIM), lambda b, h, i, pt: (step_of(b, h, i) // n_pp, 0, 0)),
                pl.BlockSpec((None, bps, N_HEADS, HEAD_DIM),
                             lambda b, h, i, pt: (step_of(b, h, i) // n_pp, step_of(b, h, i) % n_pp, 0, 0))],
            scratch_shapes=[pltpu.VMEM((n_rows, 1), F32), pltpu.VMEM((n_rows, 1), F32),
                            pltpu.VMEM((n_rows, HEAD_DIM), F32)],
        ),
        out_shape=[jax.ShapeDtypeStruct((batch * seq, D_GROUP), BF16),
                   jax.ShapeDtypeStruct((nb_s, N_HEADS, HEAD_DIM), BF16),
                   jax.ShapeDtypeStruct((nb_s, n_pages // ppb, N_HEADS, HEAD_DIM), F32)],
        compiler_params=_params("arbitrary", "arbitrary", "arbitrary"),
        name="diffsweep",
    )(page_table, slopes2, qkv_bf, qkv_bf, qkv_bf, lam4, subln_g,
      *([ckd] * pps), *([cvd] * pps), *([ckm] * pps), q16, kn16, vn16, hmask, tmpl, slope16)


def _gate_kernel(q_ref, bsum_ref, sel_ref, *, n_blk):
    kmean = bsum_ref[...] * (1.0 / MOBA_BLOCK)
    gate = jnp.sum(kmean * q_ref[...], axis=-1, keepdims=True)
    bidx = lax.broadcasted_iota(jnp.int32, gate.shape, 0).astype(F32)
    for r in range(MOBA_TOPK):
        mx = jnp.max(gate, axis=0, keepdims=True)
        first = jnp.min(jnp.where(gate == mx, bidx, float(n_blk)), axis=0, keepdims=True)
        sel_ref[r] = first[0].astype(jnp.int32)
        gate = jnp.where(bidx == first, NEG_INF, gate)


def _gate(qm8, bsum):
    nb_s, n_blk = bsum.shape[:2]
    assert n_blk >= MOBA_TOPK
    return pl.pallas_call(
        functools.partial(_gate_kernel, n_blk=n_blk),
        grid=(nb_s,),
        in_specs=[pl.BlockSpec((None, N_HEADS, HEAD_DIM), lambda b: (b, 0, 0)),
                  pl.BlockSpec((None, n_blk, N_HEADS, HEAD_DIM), lambda b: (b, 0, 0, 0))],
        out_specs=pl.BlockSpec((None, MOBA_TOPK, N_HEADS, 1), lambda b: (b, 0, 0, 0)),
        out_shape=jax.ShapeDtypeStruct((nb_s, MOBA_TOPK, N_HEADS, 1), jnp.int32),
        compiler_params=_params("arbitrary"),
        name="gate",
    )(qm8, bsum)


def _msel_kernel(sel_ref, pt_ref, slopes_ref, ck_ref, cv_ref, q_ref, kn_ref, vn_ref, g_ref, o_ref,
                 kbuf, vbuf, sem, *, past_len, nb_s):
    b = pl.program_id(0)
    ppb = MOBA_BLOCK // PAGE_SIZE

    def copies(bb, slot):
        out = []
        for h in range(N_HEADS):
            for kk in range(MOBA_TOPK):
                blk = sel_ref[bb, kk * N_HEADS + h]
                for jj in range(ppb):
                    page = pt_ref[bb, blk * ppb + jj]
                    rows = pl.ds((kk * ppb + jj) * PAGE_SIZE, PAGE_SIZE)
                    out.append(pltpu.make_async_copy(ck_ref.at[0, page, :, h, :], kbuf.at[slot, h, rows, :],
                                                     sem.at[slot]))
                    out.append(pltpu.make_async_copy(cv_ref.at[0, page, :, h, :], vbuf.at[slot, h, rows, :],
                                                     sem.at[slot]))
        return out

    @pl.when(b == 0)
    def _():
        for c in copies(0, 0):
            c.start()

    @pl.when(b + 1 < nb_s)
    def _():
        for c in copies(b + 1, (b + 1) % 2):
            c.start()

    slot = b % 2
    for c in copies(b, slot):
        c.wait()

    r_iota = lax.broadcasted_iota(jnp.int32, (1, MOBA_BLOCK), 1)
    outs = []
    for h in range(N_HEADS):
        q = q_ref[h:h + 1, :]
        k = kbuf[slot, h].astype(BF16)
        v = vbuf[slot, h].astype(BF16)
        q8 = jnp.broadcast_to(q, (8, HEAD_DIM)).astype(BF16)
        dist = jnp.concatenate(
            [past_len - (sel_ref[b, kk * N_HEADS + h] * MOBA_BLOCK + r_iota) for kk in range(MOBA_TOPK)], axis=1)
        s = _dot_nt(q8, k)[0:1] - slopes_ref[h] * dist.astype(F32)
        s_self = jnp.sum(q * kn_ref[h:h + 1, :], axis=-1, keepdims=True)
        m = jnp.maximum(jnp.max(s, axis=-1, keepdims=True), s_self)
        p = jnp.exp2(s - m)
        p_self = jnp.exp2(s_self - m)
        denom = jnp.sum(p, axis=-1, keepdims=True) + p_self
        p8 = jnp.broadcast_to(p, (8, p.shape[1])).astype(BF16)
        o = (_dot(p8, v)[0:1] + p_self * vn_ref[h:h + 1, :]) / denom
        outs.append(_head_rms(o, g_ref[...]))
    o_ref[...] = jnp.concatenate(outs, axis=0).astype(BF16)


def _msel(sel, page_table, slopes2, ckm, cvm, q_s, kn_s, vn_s, out_g, past_len):
    nb_s = page_table.shape[0]
    n_keys = MOBA_TOPK * MOBA_BLOCK
    head_spec = pl.BlockSpec((None, N_HEADS, HEAD_DIM), lambda b, sel, pt: (b, 0, 0))
    return pl.pallas_call(
        functools.partial(_msel_kernel, past_len=past_len, nb_s=nb_s),
        grid_spec=pltpu.PrefetchScalarGridSpec(
            num_scalar_prefetch=2,
            grid=(nb_s,),
            in_specs=[pl.BlockSpec(memory_space=pltpu.SMEM),
                      pl.BlockSpec(memory_space=pl.ANY), pl.BlockSpec(memory_space=pl.ANY),
                      head_spec, head_spec, head_spec,
                      pl.BlockSpec((1, HEAD_DIM), lambda b, sel, pt: (0, 0))],
            out_specs=head_spec,
            scratch_shapes=[pltpu.VMEM((2, N_HEADS, n_keys, HEAD_DIM), F32),
                            pltpu.VMEM((2, N_HEADS, n_keys, HEAD_DIM), F32),
                            pltpu.SemaphoreType.DMA((2,))],
        ),
        out_shape=jax.ShapeDtypeStruct((nb_s, N_HEADS, HEAD_DIM), BF16),
        compiler_params=_params("arbitrary"),
        name="msel",
    )(sel, page_table, slopes2, ckm, cvm, q_s, kn_s, vn_s, out_g)


def _oproj_kernel(od_ref, om_ref, x_ref, wa_ref, wb_ref, g_ref, h_ref, hn_ref):
    hval = x_ref[...] + _dot(od_ref[...], wa_ref[...]) + _dot(om_ref[...], wb_ref[...])
    h_ref[...] = hval
    ms = jnp.mean(hval * hval, axis=-1, keepdims=True)
    hn_ref[...] = (hval * lax.rsqrt(ms + EPS) * g_ref[...]).astype(BF16)


def _oproj(od, om, x, w_out_bf, mlp_g, tm):
    rows = x.shape[0]
    return pl.pallas_call(
        _oproj_kernel,
        grid=(rows // tm,),
        in_specs=[
            pl.BlockSpec((tm, D_GROUP), lambda i: (i, 0)),
            pl.BlockSpec((tm, D_GROUP), lambda i: (i, 0)),
            pl.BlockSpec((tm, D_MODEL), lambda i: (i, 0)),
            pl.BlockSpec((D_GROUP, D_MODEL), lambda i: (0, 0)),
            pl.BlockSpec((D_GROUP, D_MODEL), lambda i: (1, 0)),
            pl.BlockSpec((1, D_MODEL), lambda i: (0, 0)),
        ],
        out_specs=[pl.BlockSpec((tm, D_MODEL), lambda i: (i, 0)),
                   pl.BlockSpec((tm, D_MODEL), lambda i: (i, 0))],
        out_shape=[jax.ShapeDtypeStruct((rows, D_MODEL), F32),
                   jax.ShapeDtypeStruct((rows, D_MODEL), BF16)],
        compiler_params=_params("arbitrary"),
        name="oproj",
    )(od, om, x, w_out_bf, w_out_bf, mlp_g)


def _mlp_kernel(hn_ref, h_ref, wu_ref, wd_ref, y_ref):
    f = pl.program_id(1)
    u = jnp.maximum(_dot(hn_ref[...], wu_ref[...]), 0.0)
    contrib = _dot((u * u).astype(BF16), wd_ref[...])

    @pl.when(f == 0)
    def _():
        y_ref[...] = h_ref[...] + contrib

    @pl.when(f > 0)
    def _():
        y_ref[...] += contrib


def _mlp(hn, h, w_up_bf, w_down_bf, tm, tf):
    rows = h.shape[0]
    return pl.pallas_call(
        _mlp_kernel,
        grid=(rows // tm, D_FF // tf),
        in_specs=[
            pl.BlockSpec((tm, D_MODEL), lambda i, f: (i, 0)),
            pl.BlockSpec((tm, D_MODEL), lambda i, f: (i, 0)),
            pl.BlockSpec((D_MODEL, tf), lambda i, f: (0, f)),
            pl.BlockSpec((tf, D_MODEL), lambda i, f: (f, 0)),
        ],
        out_specs=pl.BlockSpec((tm, D_MODEL), lambda i, f: (i, 0)),
        out_shape=jax.ShapeDtypeStruct((rows, D_MODEL), F32),
        compiler_params=_params("arbitrary", "arbitrary"),
        name="mlp",
    )(hn, h, w_up_bf, w_down_bf)


def _alibi_slopes(n):
    return jnp.exp2(-8.0 * jnp.arange(1, n + 1, dtype=F32) / n)


def kernel(x_prompt, x_sample, cache_k_diff, cache_v_diff, cache_k_moba, cache_v_moba, page_table,
           attn_norm_g, w_in, qn_diff_g, kn_diff_g, qn_moba_g, kn_moba_g,
           lambda_q1, lambda_k1, lambda_q2, lambda_k2, subln_diff_g, out_norm_moba_g, w_out,
           mlp_norm_g, w_up, w_down):
    bp, sp, _ = x_prompt.shape
    bs, ss, _ = x_sample.shape
    assert ss == 1 and w_in.shape[0] == 1
    past_len = page_table.shape[1] * PAGE_SIZE
    assert past_len % MOBA_BLOCK == 0
    layer = 0

    slopes = _alibi_slopes(N_HEADS)
    slopes2 = slopes * LOG2E
    w_in_bf = w_in[layer].astype(BF16)
    w_out_bf = w_out[layer].astype(BF16)
    w_up_bf = w_up[layer].astype(BF16)
    w_down_bf = w_down[layer].astype(BF16)
    attn_g = attn_norm_g[layer].reshape(1, D_MODEL)
    mlp_g = mlp_norm_g[layer].reshape(1, D_MODEL)
    subln_g = subln_diff_g[layer].reshape(1, HEAD_DIM)
    out_g = out_norm_moba_g[layer].reshape(1, HEAD_DIM)
    lam4 = jnp.stack([lambda_q1[layer], lambda_k1[layer], lambda_q2[layer], lambda_k2[layer]]).astype(F32)

    ones = jnp.ones((HEAD_DIM,), F32)
    qd_g = jnp.tile(qn_diff_g[layer], 2)
    kd_g = jnp.tile(kn_diff_g[layer], 2)
    gains = jnp.stack([qd_g, kd_g, ones, qn_moba_g[layer], kn_moba_g[layer], ones]).reshape(6, 1, HEAD_DIM)
    scales = jnp.array([DQK_DIFF ** -0.5 * LOG2E, 1.0, 1.0, HEAD_DIM ** -0.5 * LOG2E, 1.0, 1.0], F32).reshape(6, 1, 1)
    gains_b = gains * scales
    lane = jnp.arange(HEAD_DIM)
    bm_diff = (lane[:, None] // DQK_DIFF == lane[None, :] // DQK_DIFF).astype(F32) / DQK_DIFF
    bm_moba = jnp.full((HEAD_DIM, HEAD_DIM), 1.0 / HEAD_DIM, F32)
    bmats = jnp.stack([bm_diff, bm_moba]).astype(BF16)

    def dense_tail(od, om, x2d, tm, tf):
        h, hn = _oproj(od, om, x2d, w_out_bf, mlp_g, tm)
        return _mlp(hn, h, w_up_bf, w_down_bf, tm, tf)

    xp2 = x_prompt.reshape(bp * sp, D_MODEL)
    kd_p, vd_p, qm_p, km_p, vm_p, qkv_p = _proj(xp2, attn_g, w_in_bf, gains, gains_b, bmats, tm=512)
    xs2 = x_sample.reshape(bs, D_MODEL)
    kd_s, vd_s, qm_s, km_s, vm_s, qkv_s = _proj(xs2, attn_g, w_in_bf, gains, gains_b, bmats, tm=bs)
    heads = lambda a: a.reshape(bs, N_HEADS, HEAD_DIM)
    twice = lambda a: jnp.concatenate([heads(a), heads(a)], axis=1)
    qd_s = qkv_s[:, :D_GROUP].astype(F32)
    qms_scaled = qkv_s[:, 3 * D_GROUP:4 * D_GROUP].astype(F32)

    od_p, od_s, bsum = _diffsweep(page_table, slopes2, qkv_p, cache_k_diff, cache_v_diff, cache_k_moba,
                                  twice(qd_s), twice(kd_s), twice(vd_s), lam4, subln_g, bp, sp, past_len)

    kmean_p = _kmean(km_p)
    om_p = _moba_attn(slopes2, qkv_p, qm_p, kmean_p, out_g, bp, sp)
    y_prompt = dense_tail(od_p, om_p, xp2, 512, 1024).reshape(bp, sp, D_MODEL)

    qm_gate = _qgate(xs2, attn_g, w_in[layer], qn_moba_g[layer].reshape(1, HEAD_DIM))
    sel = _gate(heads(qm_gate), bsum).reshape(bs, MOBA_TOPK * N_HEADS)
    om_s = _msel(sel, page_table, slopes2, cache_k_moba, cache_v_moba,
                 heads(qms_scaled), heads(km_s), heads(vm_s), out_g, past_len)
    y_sample = dense_tail(od_s.reshape(bs, D_GROUP), om_s.reshape(bs, D_GROUP), xs2, bs, 1024)
    y_sample = y_sample.reshape(bs, ss, D_MODEL)

    kv_p = lambda a: a.reshape(1, bp, sp, N_HEADS, HEAD_DIM)
    kv_s = lambda a: a.reshape(1, bs, ss, N_HEADS, HEAD_DIM)
    return (y_prompt, y_sample, kv_p(kd_p), kv_p(vd_p), kv_p(km_p), kv_p(vm_p),
            kv_s(kd_s), kv_s(vd_s), kv_s(km_s), kv_s(vm_s))
```
